```python
import math
import jax, jax.numpy as jnp
from jax import lax
import numpy as np

D_MODEL = 4096
BATCH = 32
SEQ = 256
DEPTH = 1
DEC_BATCH = 4
DEC_SEQ = 2048
PAST_LEN = 256

GRID_W = 64
S5_WIDTH = 1024
S5_GROUP = 16
S5_GROUPS = S5_WIDTH // S5_GROUP
S5_STATE = 64
S5_DT_MIN = 1e-3
S5_DT_MAX = 1e-1
HG_WIDTH = 2048
HG_HEADS = 16
HG_DK = HG_WIDTH // HG_HEADS
HG_DV = HG_WIDTH // HG_HEADS
HG_CHUNK = 32
N_EXPERTS = 16
EXPERT_FF = D_MODEL
EC_FACTOR = 2
DEEPNORM_ALPHA = (2 * DEPTH) ** 0.25
DEEPNORM_BETA = (8 * DEPTH) ** -0.25
LN_EPS = 1e-6
IN_COLS = S5_WIDTH + 5 * HG_WIDTH + 2 * D_MODEL

kernel_name = 'hybrid_s5_hgrn2_ec_moe_diffusion_step'


def _layer_norm(x, gain=None, bias=None):
    xf = x.astype(jnp.float32)
    mu = jnp.mean(xf, axis=-1, keepdims=True)
    var = jnp.mean(jnp.square(xf - mu), axis=-1, keepdims=True)
    y = (xf - mu) * lax.rsqrt(var + LN_EPS)
    if gain is not None:
        y = y * gain.astype(jnp.float32) + bias.astype(jnp.float32)
    return y.astype(x.dtype)


def _ada(cond, w_ada, b_ada):
    m = jax.nn.silu(cond) @ w_ada + b_ada
    return jnp.split(m[..., None, :], 6, axis=-1)


def _grid_transpose(t, rows, cols):
    b, n, ch = t.shape
    return t.reshape(b, rows, cols, ch).transpose(0, 2, 1, 3).reshape(b, n, ch)


def _s5_discretize(lam_re, lam_im, log_step, b_re, b_im):
    lam_re = lam_re.astype(jnp.float32)
    lam_im = lam_im.astype(jnp.float32)
    dt = jnp.exp(log_step.astype(jnp.float32))[:, None]
    mag = jnp.exp(lam_re * dt)
    abar_re = mag * jnp.cos(lam_im * dt)
    abar_im = mag * jnp.sin(lam_im * dt)
    nr, ni = abar_re - 1.0, abar_im
    den = lam_re * lam_re + lam_im * lam_im
    q_re = (nr * lam_re + ni * lam_im) / den
    q_im = (ni * lam_re - nr * lam_im) / den
    b_re = b_re.astype(jnp.float32)
    b_im = b_im.astype(jnp.float32)
    bbar_re = q_re[..., None] * b_re - q_im[..., None] * b_im
    bbar_im = q_re[..., None] * b_im + q_im[..., None] * b_re
    return abar_re, abar_im, bbar_re, bbar_im


def _s5_combine(e1, e2):
    a1r, a1i, b1r, b1i = e1
    a2r, a2i, b2r, b2i = e2
    return (a2r * a1r - a2i * a1i, a2r * a1i + a2i * a1r,
            a2r * b1r - a2i * b1i + b2r, a2r * b1i + a2i * b1r + b2i)


def _s5_scan(u, h0_re, h0_im, abar_re, abar_im, bbar_re, bbar_im, c_re, c_im):
    bu_re = jnp.einsum('blgh,gph->blgp', u, bbar_re)
    bu_im = jnp.einsum('blgh,gph->blgp', u, bbar_im)
    h0_re = h0_re.astype(jnp.float32)
    h0_im = h0_im.astype(jnp.float32)
    bu_re = bu_re.at[:, 0].add(abar_re * h0_re - abar_im * h0_im)
    bu_im = bu_im.at[:, 0].add(abar_re * h0_im + abar_im * h0_re)
    a_re = jnp.broadcast_to(abar_re, bu_re.shape)
    a_im = jnp.broadcast_to(abar_im, bu_im.shape)
    _, _, h_re, h_im = lax.associative_scan(_s5_combine, (a_re, a_im, bu_re, bu_im), axis=1)
    y = jnp.einsum('blgp,ghp->blgh', h_re, c_re.astype(jnp.float32)) - \
        jnp.einsum('blgp,ghp->blgh', h_im, c_im.astype(jnp.float32))
    return y, h_re[:, -1], h_im[:, -1]


def _s5_branch(u, h0_re, h0_im, lp):
    b, n, _ = u.shape
    uf = u.astype(jnp.float32).reshape(b, n, S5_GROUPS, S5_GROUP)
    y = lp['s5_d'].astype(jnp.float32) * uf
    fin_re, fin_im = [], []
    for d in range(2):
        ud = uf if d == 0 else uf[:, ::-1]
        disc = _s5_discretize(lp['s5_lam_re'][d], lp['s5_lam_im'][d], lp['s5_log_step'][d],
                              lp['s5_b_re'][d], lp['s5_b_im'][d])
        yd, hr, hi = _s5_scan(ud, h0_re[:, d], h0_im[:, d], *disc, lp['s5_c_re'][d], lp['s5_c_im'][d])
        y = y + (yd if d == 0 else yd[:, ::-1])
        fin_re.append(hr)
        fin_im.append(hi)
    y = jax.nn.gelu(y.reshape(b, n, S5_WIDTH))
    y = y * jax.nn.sigmoid(y @ lp['s5_w_glu'] + lp['s5_b_glu'])
    return y.astype(u.dtype), jnp.stack(fin_re, axis=1), jnp.stack(fin_im, axis=1)


def _hgrn2_scan(q, k, v, logf, s0):
    b, n = q.shape[:2]
    n_chunks = n // HG_CHUNK

    def to_chunks(t):
        return t.reshape(b, n_chunks, HG_CHUNK, HG_HEADS, -1).transpose(1, 0, 3, 2, 4)

    causal = jnp.tril(jnp.ones((HG_CHUNK, HG_CHUNK), dtype=bool))[:, :, None]

    def step(s, inp):
        qq, kk, vv, gg = inp
        cum = jnp.cumsum(gg, axis=2)
        o = jnp.einsum('bhtk,bhkv->bhtv', qq * jnp.exp(cum), s)
        diff = cum[:, :, :, None, :] - cum[:, :, None, :, :]
        decay = jnp.exp(jnp.where(causal, diff, -jnp.inf))
        att = jnp.einsum('bhtk,bhtsk,bhsk->bhts', qq, decay, kk)
        o = o + jnp.einsum('bhts,bhsv->bhtv', att, vv)
        last = cum[:, :, -1:, :]
        s_new = jnp.exp(last[:, :, 0, :, None]) * s + \
            jnp.einsum('bhsk,bhsv->bhkv', kk * jnp.exp(last - cum), vv)
        return s_new, o

    s_fin, oc = lax.scan(step, s0.astype(jnp.float32),
                         (to_chunks(q), to_chunks(k), to_chunks(v), to_chunks(logf)))
    o = oc.transpose(1, 0, 3, 2, 4).reshape(b, n, HG_HEADS, HG_DV)
    return o, s_fin


def _hgrn2_branch(z, s0, rows, lp):
    if rows is not None:
        z = _grid_transpose(z, rows, GRID_W)
    b, n, _ = z.shape
    zf = z.astype(jnp.float32)
    q, fz_f, fz_b, v, g = jnp.split(zf, 5, axis=-1)

    def heads(t):
        return t.reshape(b, n, HG_HEADS, -1)

    q = heads(jax.nn.silu(q))
    v = heads(v)
    lb = lp['hg_lb']
    o = jnp.zeros((b, n, HG_HEADS, HG_DV), jnp.float32)
    finals = []
    for d, fz in enumerate((fz_f, fz_b)):
        f = lb[d] + (1.0 - lb[d]) * jax.nn.sigmoid(fz)
        k = (1.0 - lb[d]) * jax.nn.sigmoid(-fz)
        args = (q, heads(k), v, heads(jnp.log(f)))
        if d == 1:
            args = tuple(t[:, ::-1] for t in args)
        od, sd = _hgrn2_scan(*args, s0[:, d])
        o = o + (od if d == 0 else od[:, ::-1])
        finals.append(sd)
    o = o * lax.rsqrt(jnp.mean(o * o, axis=-1, keepdims=True) + LN_EPS)
    o = o.reshape(b, n, HG_WIDTH) * lp['hg_norm_g'].astype(jnp.float32) * jax.nn.silu(g)
    if rows is not None:
        o = _grid_transpose(o, GRID_W, rows)
    return o.astype(z.dtype), jnp.stack(finals, axis=1)


def _mixing_sublayer(h, s5_re0, s5_im0, hg0, rows, lp):
    z = h @ lp['w_in']
    z_s5, z_hg, z_gate = jnp.split(z, [S5_WIDTH, S5_WIDTH + 5 * HG_WIDTH], axis=-1)
    y_a, s5_re, s5_im = _s5_branch(z_s5, s5_re0, s5_im0, lp)
    y_b, hg = _hgrn2_branch(z_hg, hg0, rows, lp)
    g_a, g_b = jnp.split(jax.nn.sigmoid(z_gate), 2, axis=-1)
    m = g_a * (y_a @ lp['w_proj_a']) + g_b * (y_b @ lp['w_proj_b'])
    return m @ lp['w_out'], s5_re, s5_im, hg


def _expert_choice(h, lp):
    b, n, _ = h.shape
    cap = EC_FACTOR * n // N_EXPERTS
    aff = jax.nn.softmax((h @ lp['w_router']).astype(jnp.float32), axis=-1)
    gates, idx = lax.top_k(jnp.swapaxes(aff, 1, 2), cap)
    xs = jax.vmap(lambda hb, ib: hb[ib])(h, idx)
    hid = jax.nn.silu(jnp.einsum('becd,edf->becf', xs, lp['w_gate'])) * \
        jnp.einsum('becd,edf->becf', xs, lp['w_up'])
    out = jnp.einsum('becf,efd->becd', hid, lp['w_down']) * gates[..., None].astype(h.dtype)
    return jax.vmap(lambda ib, ob: jax.ops.segment_sum(
        ob.reshape(-1, D_MODEL), ib.reshape(-1), num_segments=n))(idx, out)


def _layer(x, cond, s5_re0, s5_im0, hg0, rows, lp):
    sh1, sc1, g1, sh2, sc2, g2 = _ada(cond, lp['w_ada'], lp['b_ada'])
    h = _layer_norm(x) * (1.0 + sc1) + sh1
    mix, s5_re, s5_im, hg = _mixing_sublayer(h, s5_re0, s5_im0, hg0, rows, lp)
    x = _layer_norm(DEEPNORM_ALPHA * x + g1 * mix, lp['ln1_g'], lp['ln1_b'])
    h = _layer_norm(x) * (1.0 + sc2) + sh2
    x = _layer_norm(DEEPNORM_ALPHA * x + g2 * _expert_choice(h, lp), lp['ln2_g'], lp['ln2_b'])
    return x, s5_re, s5_im, hg


def setup_inputs(seed: int = 0) -> dict:
    key = jax.random.key(seed)
    ks = iter(jax.random.split(key, 40))

    def nrm(shape, scale):
        return scale * jax.random.normal(next(ks), shape, jnp.float32)

    L, G, P, H = DEPTH, S5_GROUPS, S5_STATE, S5_GROUP
    beta = DEEPNORM_BETA
    return {
        'x_prompt': nrm((BATCH, SEQ, D_MODEL), 1.0),
        'x_sample': nrm((DEC_BATCH, DEC_SEQ, D_MODEL), 1.0),
        'state_s5_re': nrm((DEC_BATCH, L, 2, G, P), 0.5),
        'state_s5_im': nrm((DEC_BATCH, L, 2, G, P), 0.5),
        'state_hgrn': nrm((DEC_BATCH, L, 2, HG_HEADS, HG_DK, HG_DV), 0.3),
        'c': nrm((DEC_BATCH, D_MODEL), 1.0),
        'c_ctx': nrm((D_MODEL,), 1.0),
        'w_ada': nrm((L, D_MODEL, 6 * D_MODEL), D_MODEL ** -0.5),
        'b_ada': nrm((L, 6 * D_MODEL), 0.02),
        'w_in': nrm((L, D_MODEL, IN_COLS), D_MODEL ** -0.5),
        's5_lambda_re': -0.5 + nrm((L, 2, G, P), 0.01),
        's5_lambda_im': math.pi * jnp.arange(P, dtype=jnp.float32) + nrm((L, 2, G, P), 0.01),
        's5_log_step': jax.random.uniform(next(ks), (L, 2, G), jnp.float32,
                                          math.log(S5_DT_MIN), math.log(S5_DT_MAX)),
        's5_b_re': nrm((L, 2, G, P, H), (2 * H) ** -0.5),
        's5_b_im': nrm((L, 2, G, P, H), (2 * H) ** -0.5),
        's5_c_re': nrm((L, 2, G, H, P), P ** -0.5),
        's5_c_im': nrm((L, 2, G, H, P), P ** -0.5),
        's5_d': nrm((L, G, H), 0.5),
        's5_w_glu': nrm((L, S5_WIDTH, S5_WIDTH), S5_WIDTH ** -0.5),
        's5_b_glu': nrm((L, S5_WIDTH), 0.02),
        'hg_lower_bounds': nrm((2, DEPTH + 1, HG_WIDTH), 0.1),
        'hg_norm_g': 1.0 + nrm((L, HG_WIDTH), 0.02),
        'w_proj_a': nrm((L, S5_WIDTH, D_MODEL), S5_WIDTH ** -0.5),
        'w_proj_b': nrm((L, HG_WIDTH, D_MODEL), HG_WIDTH ** -0.5),
        'w_out': nrm((L, D_MODEL, D_MODEL), beta * D_MODEL ** -0.5),
        'ln1_g': 1.0 + nrm((L, D_MODEL), 0.02),
        'ln1_b': nrm((L, D_MODEL), 0.02),
        'w_router': nrm((L, D_MODEL, N_EXPERTS), D_MODEL ** -0.5),
        'w_gate': nrm((L, N_EXPERTS, D_MODEL, EXPERT_FF), D_MODEL ** -0.5),
        'w_up': nrm((L, N_EXPERTS, D_MODEL, EXPERT_FF), D_MODEL ** -0.5),
        'w_down': nrm((L, N_EXPERTS, EXPERT_FF, D_MODEL), beta * EXPERT_FF ** -0.5),
        'ln2_g': 1.0 + nrm((L, D_MODEL), 0.02),
        'ln2_b': nrm((L, D_MODEL), 0.02),
    }


def reference(x_prompt, x_sample, state_s5_re, state_s5_im, state_hgrn, c, c_ctx,
              w_ada, b_ada, w_in, s5_lambda_re, s5_lambda_im, s5_log_step, s5_b_re, s5_b_im,
              s5_c_re, s5_c_im, s5_d, s5_w_glu, s5_b_glu, hg_lower_bounds, hg_norm_g,
              w_proj_a, w_proj_b, w_out, ln1_g, ln1_b, w_router, w_gate, w_up, w_down,
              ln2_g, ln2_b):
    rows = x_sample.shape[1] // GRID_W
    n_ctx = x_prompt.shape[0]
    zero_s5 = jnp.zeros((n_ctx, 2, S5_GROUPS, S5_STATE), jnp.float32)
    zero_hg = jnp.zeros((n_ctx, 2, HG_HEADS, HG_DK, HG_DV), jnp.float32)
    lb_all = jnp.cumsum(jax.nn.softmax(hg_lower_bounds.astype(jnp.float32), axis=1), axis=1)
    yp, ys = x_prompt, x_sample
    new_re, new_im, new_hg = [], [], []
    for l in range(DEPTH):
        lp = {
            'w_ada': w_ada[l], 'b_ada': b_ada[l], 'w_in': w_in[l],
            's5_lam_re': s5_lambda_re[l], 's5_lam_im': s5_lambda_im[l],
            's5_log_step': s5_log_step[l], 's5_b_re': s5_b_re[l], 's5_b_im': s5_b_im[l],
            's5_c_re': s5_c_re[l], 's5_c_im': s5_c_im[l], 's5_d': s5_d[l],
            's5_w_glu': s5_w_glu[l], 's5_b_glu': s5_b_glu[l],
            'hg_lb': lb_all[:, l], 'hg_norm_g': hg_norm_g[l],
            'w_proj_a': w_proj_a[l], 'w_proj_b': w_proj_b[l], 'w_out': w_out[l],
            'ln1_g': ln1_g[l], 'ln1_b': ln1_b[l],
            'w_router': w_router[l], 'w_gate': w_gate[l], 'w_up': w_up[l], 'w_down': w_down[l],
            'ln2_g': ln2_g[l], 'ln2_b': ln2_b[l],
        }
        yp, s_re, s_im, s_hg = _layer(yp, c_ctx, zero_s5, zero_s5, zero_hg, None, lp)
        new_re.append(s_re)
        new_im.append(s_im)
        new_hg.append(s_hg)
        ys, _, _, _ = _layer(ys, c, state_s5_re[:, l], state_s5_im[:, l], state_hgrn[:, l], rows, lp)
    return (yp, ys, jnp.stack(new_re, axis=1), jnp.stack(new_im, axis=1), jnp.stack(new_hg, axis=1))
```

```python
import functools
import math

import jax
import jax.numpy as jnp
from jax import lax
from jax.experimental import pallas as pl
from jax.experimental.pallas import tpu as pltpu

F32 = jnp.float32
BF16 = jnp.bfloat16
HIGHEST = lax.Precision.HIGHEST

GRID_W = 64
EC_FACTOR = 2
LN_EPS = 1e-6
S5_DT_MIN = 1e-3
S5_CHUNK = 16
HG_CHUNK = 128
LANES = 128
V7X_VMEM_LIMIT = 56 * 1024 * 1024


def _cparams(*sem):
    return pltpu.CompilerParams(dimension_semantics=sem, vmem_limit_bytes=V7X_VMEM_LIMIT)


def _pick(n, target, mult=LANES):
    if n <= target:
        return n
    t = (target // mult) * mult
    while t >= mult:
        if n % t == 0:
            return t
        t -= mult
    return n


def _layer_norm(x):
    mu = jnp.mean(x, axis=-1, keepdims=True)
    xc = x - mu
    var = jnp.mean(xc * xc, axis=-1, keepdims=True)
    return xc * lax.rsqrt(var + LN_EPS)


def _sigmoid(x):
    return jax.nn.sigmoid(x)


def _ada_kernel(c_ref, w_ref, b_ref, o_ref):
    c = c_ref[...]
    s = (c * _sigmoid(c)).astype(BF16)
    o_ref[...] = jnp.dot(s, w_ref[...].astype(BF16), preferred_element_type=F32) + b_ref[...]


def _ada_call(cond, w, b):
    m, d = cond.shape
    n = w.shape[1]
    tn = _pick(n, 512)
    return pl.pallas_call(
        _ada_kernel,
        grid=(n // tn,),
        in_specs=[pl.BlockSpec((m, d), lambda j: (0, 0)),
                  pl.BlockSpec((d, tn), lambda j: (0, j)),
                  pl.BlockSpec((1, tn), lambda j: (0, j))],
        out_specs=pl.BlockSpec((m, tn), lambda j: (0, j)),
        out_shape=jax.ShapeDtypeStruct((m, n), F32),
        compiler_params=_cparams("arbitrary"),
        name="ada",
    )(cond, w, b)


def _mod_row(i, tiles_per_seq, mod_base):
    return mod_base + i // tiles_per_seq


def _lnmm_kernel(x_ref, mod_ref, w_ref, o_ref, h_ref, *, tiles_per_seq, mod_base):
    @pl.when(pl.program_id(1) == 0)
    def _():
        row = _mod_row(pl.program_id(0), tiles_per_seq, mod_base)
        sh = mod_ref[0, pl.ds(row, 1), :]
        sc = mod_ref[1, pl.ds(row, 1), :]
        h_ref[...] = (_layer_norm(x_ref[...]) * (1.0 + sc) + sh).astype(BF16)

    o_ref[...] = jnp.dot(h_ref[...], w_ref[...], preferred_element_type=F32)


def _lnmm_call(x, mods, w, seq_len, mod_base):
    t, d = x.shape
    n = w.shape[1]
    tm = _pick(seq_len if mod_base else t, 512, 8)
    tn = _pick(n, 1024)
    tiles_per_seq = (seq_len // tm) if mod_base else (t // tm)
    kern = functools.partial(_lnmm_kernel, tiles_per_seq=tiles_per_seq, mod_base=mod_base)
    return pl.pallas_call(
        kern,
        grid=(t // tm, n // tn),
        in_specs=[pl.BlockSpec((tm, d), lambda i, j: (i, 0)),
                  pl.BlockSpec(mods.shape, lambda i, j: (0, 0, 0)),
                  pl.BlockSpec((d, tn), lambda i, j: (0, j))],
        out_specs=pl.BlockSpec((tm, tn), lambda i, j: (i, j)),
        out_shape=jax.ShapeDtypeStruct((t, n), F32),
        scratch_shapes=[pltpu.VMEM((tm, d), BF16)],
        compiler_params=_cparams("arbitrary", "arbitrary"),
        name="lnmm",
    )(x, mods, w)


def _s5_operators(lam_re, lam_im, log_step, b_re, b_im, c_re, c_im, rev):
    g, p = lam_re.shape
    hh = b_re.shape[-1]
    tt = S5_CHUNK
    dt = jnp.exp(log_step)[:, None]
    mag = jnp.exp(lam_re * dt)
    abar_re = mag * jnp.cos(lam_im * dt)
    abar_im = mag * jnp.sin(lam_im * dt)
    nr, ni = abar_re - 1.0, abar_im
    den = lam_re * lam_re + lam_im * lam_im
    q_re = (nr * lam_re + ni * lam_im) / den
    q_im = (ni * lam_re - nr * lam_im) / den
    bb_re = q_re[..., None] * b_re - q_im[..., None] * b_im
    bb_im = q_re[..., None] * b_im + q_im[..., None] * b_re
    jj = jnp.arange(tt + 1, dtype=F32)[:, None, None]
    pmag = jnp.exp(jj * lam_re * dt)
    pw_re = pmag * jnp.cos(jj * lam_im * dt)
    pw_im = pmag * jnp.sin(jj * lam_im * dt)
    ca_re = c_re[None] * pw_re[:, :, None, :] - c_im[None] * pw_im[:, :, None, :]
    ca_im = c_re[None] * pw_im[:, :, None, :] + c_im[None] * pw_re[:, :, None, :]
    kk = (jnp.einsum('jghp,gpk->jghk', ca_re[:tt], bb_re, precision=HIGHEST)
          - jnp.einsum('jghp,gpk->jghk', ca_im[:tt], bb_im, precision=HIGHEST))
    s_idx = jnp.arange(tt)[:, None]
    t_idx = jnp.arange(tt)[None, :]
    lag = (s_idx - t_idx) if rev else (t_idx - s_idx)
    m = jnp.where((lag >= 0)[:, :, None, None, None], kk[jnp.clip(lag, 0, tt - 1)], 0.0)
    m = m.transpose(2, 0, 4, 1, 3).reshape(g, tt * hh, tt * hh)
    pidx = jnp.arange(tt) if rev else (tt - 1 - jnp.arange(tt))
    ab_re = pw_re[pidx][..., None] * bb_re[None] - pw_im[pidx][..., None] * bb_im[None]
    ab_im = pw_re[pidx][..., None] * bb_im[None] + pw_im[pidx][..., None] * bb_re[None]
    ab_re = ab_re.transpose(1, 0, 3, 2).reshape(g, tt * hh, p)
    ab_im = ab_im.transpose(1, 0, 3, 2).reshape(g, tt * hh, p)
    w = jnp.concatenate([ab_re, ab_im, ab_im, ab_re], axis=-1)
    ridx = (tt - jnp.arange(tt)) if rev else (1 + jnp.arange(tt))
    v_re = ca_re[ridx].transpose(1, 3, 0, 2).reshape(g, p, tt * hh)
    v_im = ca_im[ridx].transpose(1, 3, 0, 2).reshape(g, p, tt * hh)
    v = jnp.concatenate([v_re, -v_im], axis=1)
    at_re, at_im = pw_re[tt], pw_im[tt]
    a3 = jnp.stack([jnp.concatenate([at_re, at_re], -1),
                    jnp.concatenate([-at_im, at_im], -1),
                    jnp.concatenate([at_im, -at_im], -1)], axis=1)
    a3 = jnp.pad(a3, ((0, 0), (0, 5), (0, 0)))
    return m.astype(BF16), w.astype(BF16), v.astype(BF16), a3


def _s5_kernel(u_ref, d_ref, mf_ref, mb_ref, wf_ref, wb_ref, vf_ref, vb_ref, af_ref, ab_ref,
               h0f_ref, h0fs_ref, h0b_ref, h0bs_ref,
               y_ref, finf_ref, finb_ref, s4f, s4b, hinf, hinb, *, nb, nc, p2):
    u = u_ref[0]
    ub = u.astype(BF16)
    s4f[...] = jnp.dot(ub, wf_ref[0], preferred_element_type=F32)
    s4b[...] = jnp.dot(ub, wb_ref[0], preferred_element_type=F32)

    def scan(a_ref, s4, hin, h0, h0s, reverse):
        ar = a_ref[0, 0:1, :]
        ai1 = a_ref[0, 1:2, :]
        ai2 = a_ref[0, 2:3, :]

        def body(i, carry):
            hh, hs = carry
            c = (nc - 1 - i) if reverse else i
            r0 = pl.multiple_of(c * nb, nb)
            hin[pl.ds(r0, nb), :] = hh
            s = s4[pl.ds(r0, nb), :]
            hn = hh * ar + hs * ai1 + s[:, :p2]
            hsn = hs * ar + hh * ai2 + s[:, p2:]
            return hn, hsn

        hh, _ = lax.fori_loop(0, nc, body, (h0, h0s))
        return hh

    finf_ref[0] = scan(af_ref, s4f, hinf, h0f_ref[0], h0fs_ref[0], False)
    finb_ref[0] = scan(ab_ref, s4b, hinb, h0b_ref[0], h0bs_ref[0], True)
    y = u * d_ref[0]
    y = y + jnp.dot(ub, mf_ref[0], preferred_element_type=F32)
    y = y + jnp.dot(ub, mb_ref[0], preferred_element_type=F32)
    y = y + jnp.dot(hinf[...].astype(BF16), vf_ref[0], preferred_element_type=F32)
    y = y + jnp.dot(hinb[...].astype(BF16), vb_ref[0], preferred_element_type=F32)
    y_ref[0] = y


def _s5_call(u, dvec, ops_f, ops_b, h0f, h0fs, h0b, h0bs, nb, nc):
    g, r, cw = u.shape
    mf, wf, vf, af = ops_f
    mb, wb, vb, ab = ops_b
    p2 = vf.shape[1]
    kern = functools.partial(_s5_kernel, nb=nb, nc=nc, p2=p2)
    blk = lambda shape: pl.BlockSpec((1,) + shape, lambda i: (i, 0, 0))
    return pl.pallas_call(
        kern,
        grid=(g,),
        in_specs=[blk((r, cw)), blk((1, cw)),
                  blk((cw, cw)), blk((cw, cw)), blk((cw, 2 * p2)), blk((cw, 2 * p2)),
                  blk((p2, cw)), blk((p2, cw)), blk((8, p2)), blk((8, p2)),
                  blk((nb, p2)), blk((nb, p2)), blk((nb, p2)), blk((nb, p2))],
        out_specs=[blk((r, cw)), blk((nb, p2)), blk((nb, p2))],
        out_shape=[jax.ShapeDtypeStruct((g, r, cw), F32),
                   jax.ShapeDtypeStruct((g, nb, p2), F32),
                   jax.ShapeDtypeStruct((g, nb, p2), F32)],
        scratch_shapes=[pltpu.VMEM((r, 2 * p2), F32), pltpu.VMEM((r, 2 * p2), F32),
                        pltpu.VMEM((r, p2), F32), pltpu.VMEM((r, p2), F32)],
        compiler_params=_cparams("arbitrary"),
        name="s5",
    )(u, dvec, mf, mb, wf, wb, vf, vb, af, ab, h0f, h0fs, h0b, h0bs)


def _s5_branch(z_s5, nbatch, seq, ops_f, ops_b, dvec, h0_re, h0_im):
    g = dvec.shape[0]
    hh = z_s5.shape[1] // g
    p = ops_f[2].shape[1] // 2
    nc = seq // S5_CHUNK
    nb = -(-nbatch // 8) * 8
    u = z_s5.reshape(nbatch, nc, S5_CHUNK, g, hh).transpose(3, 1, 0, 2, 4)
    if nb != nbatch:
        u = jnp.pad(u, ((0, 0), (0, 0), (0, nb - nbatch), (0, 0), (0, 0)))
    u = u.reshape(g, nc * nb, S5_CHUNK * hh)

    def init(d):
        if h0_re is None:
            z = jnp.zeros((g, nb, 2 * p), F32)
            return z, z
        hr = jnp.pad(h0_re[:, d].transpose(1, 0, 2), ((0, 0), (0, nb - nbatch), (0, 0)))
        hi = jnp.pad(h0_im[:, d].transpose(1, 0, 2), ((0, 0), (0, nb - nbatch), (0, 0)))
        return jnp.concatenate([hr, hi], -1), jnp.concatenate([hi, hr], -1)

    h0f, h0fs = init(0)
    h0b, h0bs = init(1)
    dt = jnp.tile(dvec, (1, S5_CHUNK)).reshape(g, 1, S5_CHUNK * hh)
    y, finf, finb = _s5_call(u, dt, ops_f, ops_b, h0f, h0fs, h0b, h0bs, nb, nc)
    y = y.reshape(g, nc, nb, S5_CHUNK, hh)[:, :, :nbatch].transpose(2, 1, 3, 0, 4).reshape(nbatch * seq, g * hh)
    fin = jnp.stack([finf, finb], axis=0)[:, :, :nbatch]
    fin = fin.transpose(2, 0, 1, 3)
    return y, fin[..., :p], fin[..., p:]


def _hgrn_kernel(q_ref, fz_ref, v_ref, lb_ref, s0_ref, o_ref, sfin_ref, st_ref, *, rev, nheads, dk):
    cc = q_ref.shape[1]
    c = pl.program_id(1)

    @pl.when(c == 0)
    def _():
        st_ref[...] = s0_ref[0]

    qr = q_ref[0]
    fz = fz_ref[0]
    v = v_ref[0]
    lb = lb_ref[...]
    q = qr * _sigmoid(qr)
    t = jnp.exp(-jnp.abs(fz))
    r = 1.0 / (1.0 + t)
    pos = fz >= 0.0
    sig = jnp.where(pos, r, t * r)
    nsig = jnp.where(pos, t * r, r)
    f = lb + (1.0 - lb) * sig
    k = (1.0 - lb) * nsig
    logf = jnp.log(f)
    ri = lax.broadcasted_iota(jnp.int32, (cc, cc), 0)
    ci = lax.broadcasted_iota(jnp.int32, (cc, cc), 1)
    tri = jnp.where((ci >= ri) if rev else (ci <= ri), 1.0, 0.0).astype(F32)
    cum = jnp.dot(tri, logf, precision=HIGHEST, preferred_element_type=F32)
    last = cum[0:1, :] if rev else cum[cc - 1:cc, :]
    ecum_q = (q * jnp.exp(cum)).astype(BF16)
    elast_k = (k * jnp.exp(last - cum)).astype(BF16)
    elast = jnp.exp(last)
    vb = v.astype(BF16)
    rowi = lax.broadcasted_iota(jnp.int32, (cc, 1), 0)

    levels = []
    b = cc // 2
    while b >= 1:
        target = b if rev else b - 1
        if b >= 4:
            pieces = [jnp.broadcast_to(cum[i * 2 * b + target:i * 2 * b + target + 1, :], (2 * b, cum.shape[1]))
                      for i in range(cc // (2 * b))]
            rb = pieces[0] if len(pieces) == 1 else jnp.concatenate(pieces, axis=0)
        else:
            pin = rowi % (2 * b)
            rb = cum
            for p in range(2 * b):
                delta = target - p
                if delta == 0:
                    continue
                shifted = pltpu.roll(cum, (-delta) % cc, axis=0)
                rb = jnp.where(pin == p, shifted, rb)
        x = jnp.exp(-jnp.abs(cum - rb))
        upper = ((rowi // b) % 2) == 1
        qmask = jnp.logical_not(upper) if rev else upper
        qx = jnp.where(qmask, q * x, 0.0).astype(BF16)
        kx = jnp.where(qmask, 0.0, k * x).astype(BF16)
        bm = None
        if 2 * b < cc:
            bm = jnp.where((ri // (2 * b)) == (ci // (2 * b)), 1.0, 0.0).astype(F32)
        levels.append((qx, kx, bm))
        b //= 2

    nt = (((1,), (1,)), ((), ()))
    for h in range(nheads):
        sl = slice(h * dk, (h + 1) * dk)
        att = None
        for qx, kx, bm in levels:
            a = lax.dot_general(qx[:, sl], kx[:, sl], nt, preferred_element_type=F32)
            if bm is not None:
                a = a * bm
            att = a if att is None else att + a
        dg = jnp.sum(q[:, sl] * k[:, sl], axis=-1, keepdims=True)
        st = st_ref[h]
        o = lax.dot_general(ecum_q[:, sl], st.astype(BF16), nt, preferred_element_type=F32)
        o = o + jnp.dot(att.astype(BF16), vb[:, sl], preferred_element_type=F32) + dg * v[:, sl]
        o_ref[0, :, sl] = o
        vt = v[:, sl].T.astype(BF16)
        st_ref[h] = st * elast[:, sl] + jnp.dot(vt, elast_k[:, sl], preferred_element_type=F32)

    @pl.when(c == pl.num_programs(1) - 1)
    def _():
        sfin_ref[0] = st_ref[...]


def _hgrn_call(zq, q_blk, fz_blk, v_blk, lb, s0t, nbatch, seq, rev):
    nheads, dv, dk = s0t.shape[1:]
    w = nheads * dk
    cc = min(HG_CHUNK, seq)
    nc = seq // cc
    cidx = (lambda c: nc - 1 - c) if rev else (lambda c: c)
    kern = functools.partial(_hgrn_kernel, rev=rev, nheads=nheads, dk=dk)
    zspec = lambda blk: pl.BlockSpec((1, cc, w), lambda b, c: (b, cidx(c), blk))
    return pl.pallas_call(
        kern,
        grid=(nbatch, nc),
        in_specs=[zspec(q_blk), zspec(fz_blk), zspec(v_blk),
                  pl.BlockSpec((1, w), lambda b, c: (0, 0)),
                  pl.BlockSpec((1, nheads, dv, dk), lambda b, c: (b, 0, 0, 0))],
        out_specs=[pl.BlockSpec((1, cc, w), lambda b, c: (b, cidx(c), 0)),
                   pl.BlockSpec((1, nheads, dv, dk), lambda b, c: (b, 0, 0, 0))],
        out_shape=[jax.ShapeDtypeStruct((nbatch, seq, w), F32),
                   jax.ShapeDtypeStruct((nbatch, nheads, dv, dk), F32)],
        scratch_shapes=[pltpu.VMEM((nheads, dv, dk), F32)],
        compiler_params=_cparams("arbitrary", "arbitrary"),
        name="hgrn_bwd" if rev else "hgrn_fwd",
    )(zq, zq, zq, lb, s0t)


def _merge1_kernel(y_ref, of_ref, ob_ref, g_ref, ga_ref, gb_ref, wglu_ref, bglu_ref, ng_ref, wa_ref, wb_ref,
                   m_ref, ya_s, yb_s, *, nheads):
    @pl.when(pl.program_id(1) == 0)
    def _():
        y = jax.nn.gelu(y_ref[...])
        glu = jnp.dot(y.astype(BF16), wglu_ref[...], preferred_element_type=F32) + bglu_ref[...]
        ya_s[...] = (y * _sigmoid(glu)).astype(BF16)
        dk = of_ref.shape[1] // nheads
        for h in range(nheads):
            sl = slice(h * dk, (h + 1) * dk)
            o = of_ref[:, sl] + ob_ref[:, sl]
            ms = jnp.mean(o * o, axis=-1, keepdims=True)
            g = g_ref[:, sl]
            yb_s[:, sl] = (o * lax.rsqrt(ms + LN_EPS) * ng_ref[:, sl] * (g * _sigmoid(g))).astype(BF16)

    pa = jnp.dot(ya_s[...], wa_ref[...], preferred_element_type=F32)
    pb = jnp.dot(yb_s[...], wb_ref[...], preferred_element_type=F32)
    m_ref[...] = (_sigmoid(ga_ref[...]) * pa + _sigmoid(gb_ref[...]) * pb).astype(BF16)


def _merge1_call(y_s5, o_f, o_b, z, cols, wglu, bglu, ng, wa, wb, nheads):
    t, sw = y_s5.shape
    hw = o_f.shape[1]
    d = wa.shape[1]
    tm = _pick(t, 256, 8)
    tn = _pick(d, 1024)
    nj = d // tn
    g_blk = cols['g'] // hw
    ga0 = cols['ga'] // tn
    gb0 = cols['gb'] // tn
    kern = functools.partial(_merge1_kernel, nheads=nheads)
    return pl.pallas_call(
        kern,
        grid=(t // tm, nj),
        in_specs=[pl.BlockSpec((tm, sw), lambda i, j: (i, 0)),
                  pl.BlockSpec((tm, hw), lambda i, j: (i, 0)),
                  pl.BlockSpec((tm, hw), lambda i, j: (i, 0)),
                  pl.BlockSpec((tm, hw), lambda i, j: (i, g_blk)),
                  pl.BlockSpec((tm, tn), lambda i, j: (i, ga0 + j)),
                  pl.BlockSpec((tm, tn), lambda i, j: (i, gb0 + j)),
                  pl.BlockSpec((sw, sw), lambda i, j: (0, 0)),
                  pl.BlockSpec((1, sw), lambda i, j: (0, 0)),
                  pl.BlockSpec((1, hw), lambda i, j: (0, 0)),
                  pl.BlockSpec((sw, tn), lambda i, j: (0, j)),
                  pl.BlockSpec((hw, tn), lambda i, j: (0, j))],
        out_specs=pl.BlockSpec((tm, tn), lambda i, j: (i, j)),
        out_shape=jax.ShapeDtypeStruct((t, d), BF16),
        scratch_shapes=[pltpu.VMEM((tm, sw), BF16), pltpu.VMEM((tm, hw), BF16)],
        compiler_params=_cparams("arbitrary", "arbitrary"),
        name="merge1",
    )(y_s5, o_f, o_b, z, z, z, wglu, bglu, ng, wa, wb)


def _merge2_kernel(m_ref, w_ref, x_ref, mod_ref, lng_ref, lnb_ref, wr_ref, x1_ref, h2_ref, lg_ref, acc_ref,
                   *, nj, tiles_per_seq, mod_base, alpha):
    j = pl.program_id(1)
    acc_ref[j] = jnp.dot(m_ref[...], w_ref[...], preferred_element_type=F32)

    @pl.when(j == nj - 1)
    def _():
        row = _mod_row(pl.program_id(0), tiles_per_seq, mod_base)
        g1 = mod_ref[2, pl.ds(row, 1), :]
        sh2 = mod_ref[3, pl.ds(row, 1), :]
        sc2 = mod_ref[4, pl.ds(row, 1), :]
        mix = jnp.concatenate([acc_ref[jj] for jj in range(nj)], axis=-1) if nj > 1 else acc_ref[0]
        x1 = _layer_norm(alpha * x_ref[...] + g1 * mix) * lng_ref[...] + lnb_ref[...]
        x1_ref[...] = x1
        h2 = _layer_norm(x1) * (1.0 + sc2) + sh2
        h2_ref[...] = h2.astype(BF16)
        lg_ref[...] = jnp.dot(h2, wr_ref[...], precision=HIGHEST, preferred_element_type=F32)


def _merge2_call(m, w_out, x, mods, lng, lnb, wr, seq_len, mod_base, alpha):
    t, d = x.shape
    tm = _pick(seq_len if mod_base else t, 256, 8)
    tn = _pick(d, 1024)
    nj = d // tn
    tiles_per_seq = (seq_len // tm) if mod_base else (t // tm)
    kern = functools.partial(_merge2_kernel, nj=nj, tiles_per_seq=tiles_per_seq, mod_base=mod_base, alpha=alpha)
    return pl.pallas_call(
        kern,
        grid=(t // tm, nj),
        in_specs=[pl.BlockSpec((tm, d), lambda i, j: (i, 0)),
                  pl.BlockSpec((d, tn), lambda i, j: (0, j)),
                  pl.BlockSpec((tm, d), lambda i, j: (i, 0)),
                  pl.BlockSpec(mods.shape, lambda i, j: (0, 0, 0)),
                  pl.BlockSpec((1, d), lambda i, j: (0, 0)),
                  pl.BlockSpec((1, d), lambda i, j: (0, 0)),
                  pl.BlockSpec((d, LANES), lambda i, j: (0, 0))],
        out_specs=[pl.BlockSpec((tm, d), lambda i, j: (i, 0)),
                   pl.BlockSpec((tm, d), lambda i, j: (i, 0)),
                   pl.BlockSpec((tm, LANES), lambda i, j: (i, 0))],
        out_shape=[jax.ShapeDtypeStruct((t, d), F32),
                   jax.ShapeDtypeStruct((t, d), BF16),
                   jax.ShapeDtypeStruct((t, LANES), F32)],
        scratch_shapes=[pltpu.VMEM((nj, tm, tn), F32)],
        compiler_params=_cparams("arbitrary", "arbitrary"),
        name="merge2",
    )(m, w_out, x, mods, lng, lnb, wr)


def _route_kernel(lg_ref, pos_ref, posc_ref, gatec_ref, aff_s, afft_s, cnt_s, *, ne, cap):
    n = lg_ref.shape[0]
    lg = lg_ref[...]
    lane = lax.broadcasted_iota(jnp.int32, (n, LANES), 1)
    lmask = jnp.where(lane < ne, lg, -jnp.inf)
    mx = jnp.max(lmask, axis=-1, keepdims=True)
    ex = jnp.exp(lmask - mx)
    aff = ex / jnp.sum(ex, axis=-1, keepdims=True)
    aff_s[...] = aff
    afft_s[...] = aff.T
    cnt_s[...] = jnp.zeros_like(cnt_s)
    sb = min(n, 64)
    tl = lax.broadcasted_iota(jnp.int32, (1, n), 1)

    def count(i, carry):
        s0 = pl.multiple_of(i * sb, sb)
        blk = aff_s[pl.ds(s0, sb), :]
        si = s0 + lax.broadcasted_iota(jnp.int32, (sb, 1), 0)
        tie = jnp.where(si < tl, 1.0, 0.0)
        for e in range(ne):
            col = blk[:, e:e + 1]
            row = afft_s[e:e + 1, :]
            beats = jnp.where(col > row, 1.0, jnp.where(col == row, tie, 0.0))
            cnt_s[e:e + 1, :] += jnp.sum(beats, axis=0, keepdims=True)
        return carry

    lax.fori_loop(0, n // sb, count, 0)
    sel = jnp.where(cnt_s[...] < cap, 1.0, 0.0)
    selb = sel.astype(BF16)
    pb = min(n, 256)
    ui = lax.broadcasted_iota(jnp.int32, (n, pb), 0)
    uj = lax.broadcasted_iota(jnp.int32, (n, pb), 1)
    pos_blocks = []
    for t0 in range(0, n, pb):
        upper = jnp.where(ui < uj + t0, 1.0, 0.0).astype(BF16)
        pos_blocks.append(jnp.dot(selb, upper, preferred_element_type=F32))
    pos = jnp.concatenate(pos_blocks, axis=-1) if len(pos_blocks) > 1 else pos_blocks[0]
    pos = jnp.where(sel > 0.0, pos, -1.0)
    gate = sel * afft_s[0:ne, :]
    pos_ref[0] = pos.astype(jnp.int32)
    pad = jnp.zeros((LANES - ne, n), F32)
    posc_ref[...] = jnp.concatenate([pos, pad], axis=0).T
    gatec_ref[...] = jnp.concatenate([gate, pad], axis=0).T


def _route_call(lg, nsets, n, ne, cap):
    kern = functools.partial(_route_kernel, ne=ne, cap=cap)
    return pl.pallas_call(
        kern,
        grid=(nsets,),
        in_specs=[pl.BlockSpec((n, LANES), lambda s: (s, 0))],
        out_specs=[pl.BlockSpec((1, ne, n), lambda s: (s, 0, 0)),
                   pl.BlockSpec((n, LANES), lambda s: (s, 0)),
                   pl.BlockSpec((n, LANES), lambda s: (s, 0))],
        out_shape=[jax.ShapeDtypeStruct((nsets, ne, n), jnp.int32),
                   jax.ShapeDtypeStruct((nsets * n, LANES), F32),
                   jax.ShapeDtypeStruct((nsets * n, LANES), F32)],
        scratch_shapes=[pltpu.VMEM((n, LANES), F32), pltpu.VMEM((LANES, n), F32), pltpu.VMEM((ne, n), F32)],
        compiler_params=_cparams("arbitrary"),
        name="route",
    )(lg)


def _gather_kernel(pos_ref, h_ref, x_ref, *, ne, cap):
    n = h_ref.shape[0]
    pos = pos_ref[0]
    h = h_ref[...]
    jidx = lax.broadcasted_iota(jnp.int32, (cap, n), 0)
    onehot = lambda e: jnp.where(pos[e:e + 1, :] == jidx, 1.0, 0.0).astype(BF16)
    if cap * ne <= 1024:
        p = jnp.concatenate([onehot(e) for e in range(ne)], axis=0)
        x = jnp.dot(p, h, preferred_element_type=F32).astype(BF16)
        for e in range(ne):
            x_ref[e] = x[e * cap:(e + 1) * cap]
    else:
        for e in range(ne):
            x_ref[e] = jnp.dot(onehot(e), h, preferred_element_type=F32).astype(BF16)


def _gather_call(pos, h2, nsets, n, ne, cap):
    d = h2.shape[1]
    td = _pick(d, 1024)
    kern = functools.partial(_gather_kernel, ne=ne, cap=cap)
    return pl.pallas_call(
        kern,
        grid=(nsets, d // td),
        in_specs=[pl.BlockSpec((1, ne, n), lambda s, j: (s, 0, 0)),
                  pl.BlockSpec((n, td), lambda s, j: (s, j))],
        out_specs=pl.BlockSpec((ne, cap, td), lambda s, j: (0, s, j)),
        out_shape=jax.ShapeDtypeStruct((ne, nsets * cap, d), BF16),
        compiler_params=_cparams("arbitrary", "arbitrary"),
        name="gather",
    )(pos, h2)


def _ffn_kernel(x_ref, wg_ref, wu_ref, wd_ref, y_ref, hid_ref, *, nf):
    j = pl.program_id(1)

    @pl.when(j < nf)
    def _():
        x = x_ref[0]
        a = jnp.dot(x, wg_ref[0].astype(BF16), preferred_element_type=F32)
        b = jnp.dot(x, wu_ref[0].astype(BF16), preferred_element_type=F32)
        hid_ref[j] = (a * _sigmoid(a) * b).astype(BF16)

    @pl.when(j >= nf)
    def _():
        hid = jnp.concatenate([hid_ref[jj] for jj in range(nf)], axis=-1) if nf > 1 else hid_ref[0]
        y_ref[0] = jnp.dot(hid, wd_ref[0].astype(BF16), preferred_element_type=F32).astype(BF16)


def _ffn_call(xg, w_gate, w_up, w_down):
    ne, r, d = xg.shape
    ff = w_gate.shape[2]
    tf = _pick(ff, 256)
    tn = _pick(d, 256)
    nf = ff // tf
    nn = d // tn
    kern = functools.partial(_ffn_kernel, nf=nf)
    return pl.pallas_call(
        kern,
        grid=(ne, nf + nn),
        in_specs=[pl.BlockSpec((1, r, d), lambda e, j: (e, 0, 0), pipeline_mode=pl.Buffered(1)),
                  pl.BlockSpec((1, d, tf), lambda e, j: (e, 0, jnp.minimum(j, nf - 1))),
                  pl.BlockSpec((1, d, tf), lambda e, j: (e, 0, jnp.minimum(j, nf - 1))),
                  pl.BlockSpec((1, ff, tn), lambda e, j: (e, 0, jnp.maximum(j - nf, 0)))],
        out_specs=pl.BlockSpec((1, r, tn), lambda e, j: (e, 0, jnp.maximum(j - nf, 0))),
        out_shape=jax.ShapeDtypeStruct((ne, r, d), BF16),
        scratch_shapes=[pltpu.VMEM((nf, r, tf), BF16)],
        compiler_params=_cparams("arbitrary", "arbitrary"),
        name="ffn",
    )(xg, w_gate, w_up, w_down)


def _combine_kernel(posc_ref, gatec_ref, y_ref, x1_ref, mod_ref, lng_ref, lnb_ref, o_ref, acc_ref,
                    *, ke, cap, nk, tiles_per_seq, mod_base, alpha):
    k = pl.program_id(2)

    @pl.when(k == 0)
    def _():
        acc_ref[...] = jnp.zeros_like(acc_ref)

    lane = lax.broadcasted_iota(jnp.int32, (1, LANES), 1)
    posc = posc_ref[...]
    gatec = gatec_ref[...]

    def column(ee):
        m = lane == k * ke + ee
        return (jnp.sum(jnp.where(m, posc, 0.0), axis=-1, keepdims=True),
                jnp.sum(jnp.where(m, gatec, 0.0), axis=-1, keepdims=True))

    pieces = []
    if cap >= LANES:
        for ee in range(ke):
            pc, gc = column(ee)
            for c0 in range(0, cap, LANES):
                pieces.append(jnp.where(pc == (lane + c0).astype(F32), gc, 0.0))
    else:
        per = LANES // cap
        for e0 in range(0, ke, per):
            pb, gb = column(e0)
            for q in range(1, per):
                pq, gq = column(e0 + q)
                inq = lane >= q * cap
                pb = jnp.where(inq, pq, pb)
                gb = jnp.where(inq, gq, gb)
            pieces.append(jnp.where(pb == (lane % cap).astype(F32), gb, 0.0))
    gt = (jnp.concatenate(pieces, axis=-1) if len(pieces) > 1 else pieces[0]).astype(BF16)
    y = y_ref[...].reshape(ke * cap, y_ref.shape[-1])
    acc_ref[...] += jnp.dot(gt, y, preferred_element_type=F32)

    @pl.when(k == nk - 1)
    def _():
        row = _mod_row(pl.program_id(0) * pl.num_programs(1) + pl.program_id(1), tiles_per_seq, mod_base)
        g2 = mod_ref[5, pl.ds(row, 1), :]
        o_ref[...] = _layer_norm(alpha * x1_ref[...] + g2 * acc_ref[...]) * lng_ref[...] + lnb_ref[...]


def _combine_call(posc, gatec, y, x1, mods, lng, lnb, nsets, n, ne, cap, mod_base, alpha):
    t, d = x1.shape
    tq = _pick(n, 256, 8)
    nti = n // tq
    ke = max(1, min(ne, 512 // cap))
    nk = ne // ke
    tiles_per_seq = nti if mod_base else (t // tq)
    kern = functools.partial(_combine_kernel, ke=ke, cap=cap, nk=nk, tiles_per_seq=tiles_per_seq,
                             mod_base=mod_base, alpha=alpha)
    return pl.pallas_call(
        kern,
        grid=(nsets, nti, nk),
        in_specs=[pl.BlockSpec((tq, LANES), lambda s, i, k: (s * nti + i, 0)),
                  pl.BlockSpec((tq, LANES), lambda s, i, k: (s * nti + i, 0)),
                  pl.BlockSpec((ke, cap, d), lambda s, i, k: (k, s, 0)),
                  pl.BlockSpec((tq, d), lambda s, i, k: (s * nti + i, 0)),
                  pl.BlockSpec(mods.shape, lambda s, i, k: (0, 0, 0)),
                  pl.BlockSpec((1, d), lambda s, i, k: (0, 0)),
                  pl.BlockSpec((1, d), lambda s, i, k: (0, 0))],
        out_specs=pl.BlockSpec((tq, d), lambda s, i, k: (s * nti + i, 0)),
        out_shape=jax.ShapeDtypeStruct((t, d), F32),
        scratch_shapes=[pltpu.VMEM((tq, d), F32)],
        compiler_params=_cparams("arbitrary", "arbitrary", "arbitrary"),
        name="combine",
    )(posc, gatec, y, x1, mods, lng, lnb)


def _group_forward(x, nbatch, seq, mod_base, rows, mods, prm, s5_h0, hg_s0t, alpha):
    d = x.shape[-1]
    x2 = x.reshape(nbatch * seq, d)
    cols = prm['cols']
    hw = prm['hw']
    z = _lnmm_call(x2, mods, prm['w_in'], seq, mod_base)

    y_s5, fin_re, fin_im = _s5_branch(z[:, cols['s5']:], nbatch, seq, prm['s5_f'], prm['s5_b'], prm['s5_d'],
                                      None if s5_h0 is None else s5_h0[0], None if s5_h0 is None else s5_h0[1])

    ncol = z.shape[1]
    if rows is None:
        zq = z.reshape(nbatch, seq, ncol)
        qb, ffb, fbb, vb = (cols[k] // hw for k in ('q', 'ff', 'fb', 'v'))
    else:
        zs = z[:, cols['q']:cols['q'] + 4 * hw].reshape(nbatch, rows, GRID_W, 4 * hw)
        zq = zs.transpose(0, 2, 1, 3).reshape(nbatch, seq, 4 * hw)
        qb, ffb, fbb, vb = 0, 1, 2, 3
    o_f, sfin_f = _hgrn_call(zq, qb, ffb, vb, prm['lb'][0], hg_s0t[:, 0], nbatch, seq, False)
    o_b, sfin_b = _hgrn_call(zq, qb, fbb, vb, prm['lb'][1], hg_s0t[:, 1], nbatch, seq, True)
    if rows is not None:
        unt = lambda o: o.reshape(nbatch, GRID_W, rows, hw).transpose(0, 2, 1, 3)
        o_f, o_b = unt(o_f), unt(o_b)
    o_f = o_f.reshape(nbatch * seq, hw)
    o_b = o_b.reshape(nbatch * seq, hw)

    m = _merge1_call(y_s5, o_f, o_b, z, cols, prm['w_glu'], prm['b_glu'], prm['hg_norm_g'],
                     prm['w_proj_a'], prm['w_proj_b'], prm['nheads'])
    x1, h2, lg = _merge2_call(m, prm['w_out'], x2, mods, prm['ln1_g'], prm['ln1_b'], prm['w_router'],
                              seq, mod_base, alpha)
    ne = prm['ne']
    cap = EC_FACTOR * seq // ne
    pos, posc, gatec = _route_call(lg, nbatch, seq, ne, cap)
    xg = _gather_call(pos, h2, nbatch, seq, ne, cap)
    yg = _ffn_call(xg, prm['w_gate'], prm['w_up'], prm['w_down'])
    out = _combine_call(posc, gatec, yg, x1, mods, prm['ln2_g'], prm['ln2_b'], nbatch, seq, ne, cap,
                        mod_base, alpha)
    hg_fin = jnp.stack([sfin_f, sfin_b], axis=1)
    return out.reshape(nbatch, seq, d), fin_re, fin_im, jnp.swapaxes(hg_fin, -1, -2)


def kernel(x_prompt, x_sample, state_s5_re, state_s5_im, state_hgrn, c, c_ctx, w_ada, b_ada, w_in, s5_lambda_re, s5_lambda_im, s5_log_step, s5_b_re, s5_b_im, s5_c_re, s5_c_im, s5_d, s5_w_glu, s5_b_glu, hg_lower_bounds, hg_norm_g, w_proj_a, w_proj_b, w_out, ln1_g, ln1_b, w_router, w_gate, w_up, w_down, ln2_g, ln2_b):
    depth = w_ada.shape[0]
    assert depth == 1, "single shared layer"
    nb_p, seq_p, d = x_prompt.shape
    nb_s, seq_s, _ = x_sample.shape
    g, p = s5_lambda_re.shape[2:]
    sw = s5_w_glu.shape[1]
    nheads, dk, dv = state_hgrn.shape[3:]
    hw = nheads * dk
    ne = w_gate.shape[1]
    alpha = (2 * depth) ** 0.25
    l = 0

    cond = jnp.concatenate([c_ctx[None], c, jnp.zeros((8 - 1 - nb_s, d), F32)], axis=0)
    mods = _ada_call(cond, w_ada[l], b_ada[l][None])
    mods = mods.reshape(8, 6, d).transpose(1, 0, 2)

    w_in_l = w_in[l]
    w_in_r = jnp.concatenate([w_in_l[:, sw + 5 * hw:], w_in_l[:, sw:sw + 5 * hw], w_in_l[:, :sw]], axis=1).astype(BF16)
    cols = {'ga': 0, 'gb': d, 'q': 2 * d, 'ff': 2 * d + hw, 'fb': 2 * d + 2 * hw, 'v': 2 * d + 3 * hw,
            'g': 2 * d + 4 * hw, 's5': 2 * d + 5 * hw}

    lb_all = jnp.cumsum(jax.nn.softmax(hg_lower_bounds.astype(F32), axis=1), axis=1)[:, l]
    ops = [_s5_operators(s5_lambda_re[l, dd], s5_lambda_im[l, dd], s5_log_step[l, dd], s5_b_re[l, dd],
                         s5_b_im[l, dd], s5_c_re[l, dd], s5_c_im[l, dd], dd == 1) for dd in range(2)]
    prm = {
        'cols': cols, 'hw': hw, 'nheads': nheads, 'ne': ne,
        'w_in': w_in_r, 's5_f': ops[0], 's5_b': ops[1], 's5_d': s5_d[l],
        'lb': lb_all[:, None, :],
        'w_glu': s5_w_glu[l].astype(BF16), 'b_glu': s5_b_glu[l][None], 'hg_norm_g': hg_norm_g[l][None],
        'w_proj_a': w_proj_a[l].astype(BF16), 'w_proj_b': w_proj_b[l].astype(BF16),
        'w_out': w_out[l].astype(BF16), 'ln1_g': ln1_g[l][None], 'ln1_b': ln1_b[l][None],
        'w_router': jnp.pad(w_router[l], ((0, 0), (0, LANES - ne))),
        'w_gate': w_gate[l], 'w_up': w_up[l], 'w_down': w_down[l],
        'ln2_g': ln2_g[l][None], 'ln2_b': ln2_b[l][None],
    }
    zero_hg = jnp.zeros((nb_p, 2, nheads, dv, dk), F32)
    yp, re_p, im_p, hg_p = _group_forward(x_prompt, nb_p, seq_p, 0, None, mods, prm, None, zero_hg, alpha)
    s0t = jnp.swapaxes(state_hgrn[:, l], -1, -2)
    ys, _, _, _ = _group_forward(x_sample, nb_s, seq_s, 1, seq_s // GRID_W, mods, prm,
                                 (state_s5_re[:, l], state_s5_im[:, l]), s0t, alpha)
    return yp, ys, re_p[:, None], im_p[:, None], hg_p[:, None]
```

```python
import functools
import math

import jax
import jax.numpy as jnp
from jax import lax
from jax.experimental import pallas as pl
from jax.experimental.pallas import tpu as pltpu

F32 = jnp.float32
BF16 = jnp.bfloat16
HIGHEST = lax.Precision.HIGHEST

GRID_W = 64
EC_FACTOR = 2
LN_EPS = 1e-6
S5_DT_MIN = 1e-3
S5_CHUNK = 16
HG_CHUNK = 128
LANES = 128
V7X_VMEM_LIMIT = 56 * 1024 * 1024


def _cparams(*sem):
    return pltpu.CompilerParams(dimension_semantics=sem, vmem_limit_bytes=V7X_VMEM_LIMIT)


def _pick(n, target, mult=LANES):
    if n <= target:
        return n
    t = (target // mult) * mult
    while t >= mult:
        if n % t == 0:
            return t
        t -= mult
    return n


def _layer_norm(x):
    mu = jnp.mean(x, axis=-1, keepdims=True)
    xc = x - mu
    var = jnp.mean(xc * xc, axis=-1, keepdims=True)
    return xc * lax.rsqrt(var + LN_EPS)


def _sigmoid(x):
    return jax.nn.sigmoid(x)


def _ada_kernel(c_ref, w_ref, b_ref, o_ref):
    c = c_ref[...]
    s = (c * _sigmoid(c)).astype(BF16)
    o_ref[...] = jnp.dot(s, w_ref[...].astype(BF16), preferred_element_type=F32) + b_ref[...]


def _ada_call(cond, w, b):
    m, d = cond.shape
    n = w.shape[1]
    tn = _pick(n, 512)
    return pl.pallas_call(
        _ada_kernel,
        grid=(n // tn,),
        in_specs=[pl.BlockSpec((m, d), lambda j: (0, 0)),
                  pl.BlockSpec((d, tn), lambda j: (0, j)),
                  pl.BlockSpec((1, tn), lambda j: (0, j))],
        out_specs=pl.BlockSpec((m, tn), lambda j: (0, j)),
        out_shape=jax.ShapeDtypeStruct((m, n), F32),
        compiler_params=_cparams("arbitrary"),
        name="ada",
    )(cond, w, b)


def _mod_row(i, tiles_per_seq, mod_base):
    return mod_base + i // tiles_per_seq


def _lnmm_kernel(x_ref, mod_ref, w_ref, o_ref, h_ref, *, tiles_per_seq, mod_base):
    @pl.when(pl.program_id(1) == 0)
    def _():
        row = _mod_row(pl.program_id(0), tiles_per_seq, mod_base)
        sh = mod_ref[0, pl.ds(row, 1), :]
        sc = mod_ref[1, pl.ds(row, 1), :]
        h_ref[...] = (_layer_norm(x_ref[...]) * (1.0 + sc) + sh).astype(BF16)

    o_ref[...] = jnp.dot(h_ref[...], w_ref[...], preferred_element_type=F32)


def _lnmm_call(x, mods, w, seq_len, mod_base):
    t, d = x.shape
    n = w.shape[1]
    tm = _pick(seq_len if mod_base else t, 512, 8)
    tn = _pick(n, 1024)
    tiles_per_seq = (seq_len // tm) if mod_base else (t // tm)
    kern = functools.partial(_lnmm_kernel, tiles_per_seq=tiles_per_seq, mod_base=mod_base)
    return pl.pallas_call(
        kern,
        grid=(t // tm, n // tn),
        in_specs=[pl.BlockSpec((tm, d), lambda i, j: (i, 0)),
                  pl.BlockSpec(mods.shape, lambda i, j: (0, 0, 0)),
                  pl.BlockSpec((d, tn), lambda i, j: (0, j))],
        out_specs=pl.BlockSpec((tm, tn), lambda i, j: (i, j)),
        out_shape=jax.ShapeDtypeStruct((t, n), F32),
        scratch_shapes=[pltpu.VMEM((tm, d), BF16)],
        compiler_params=_cparams("arbitrary", "arbitrary"),
        name="lnmm",
    )(x, mods, w)


def _s5_operators(lam_re, lam_im, log_step, b_re, b_im, c_re, c_im, rev):
    g, p = lam_re.shape
    hh = b_re.shape[-1]
    tt = S5_CHUNK
    dt = jnp.exp(log_step)[:, None]
    mag = jnp.exp(lam_re * dt)
    abar_re = mag * jnp.cos(lam_im * dt)
    abar_im = mag * jnp.sin(lam_im * dt)
    nr, ni = abar_re - 1.0, abar_im
    den = lam_re * lam_re + lam_im * lam_im
    q_re = (nr * lam_re + ni * lam_im) / den
    q_im = (ni * lam_re - nr * lam_im) / den
    bb_re = q_re[..., None] * b_re - q_im[..., None] * b_im
    bb_im = q_re[..., None] * b_im + q_im[..., None] * b_re
    jj = jnp.arange(tt + 1, dtype=F32)[:, None, None]
    pmag = jnp.exp(jj * lam_re * dt)
    pw_re = pmag * jnp.cos(jj * lam_im * dt)
    pw_im = pmag * jnp.sin(jj * lam_im * dt)
    ca_re = c_re[None] * pw_re[:, :, None, :] - c_im[None] * pw_im[:, :, None, :]
    ca_im = c_re[None] * pw_im[:, :, None, :] + c_im[None] * pw_re[:, :, None, :]
    kk = (jnp.einsum('jghp,gpk->jghk', ca_re[:tt], bb_re, precision=HIGHEST)
          - jnp.einsum('jghp,gpk->jghk', ca_im[:tt], bb_im, precision=HIGHEST))
    s_idx = jnp.arange(tt)[:, None]
    t_idx = jnp.arange(tt)[None, :]
    lag = (s_idx - t_idx) if rev else (t_idx - s_idx)
    sel = (lag[None] == jnp.arange(tt)[:, None, None]).astype(F32)
    m = jnp.einsum('jst,jghk->gskth', sel, kk, precision=HIGHEST).reshape(g, tt * hh, tt * hh)
    pg_re = pw_re.transpose(1, 0, 2)
    pg_im = pw_im.transpose(1, 0, 2)
    pidx = jnp.arange(tt) if rev else (tt - 1 - jnp.arange(tt))
    pr, pi = pg_re[:, pidx][:, :, None, :], pg_im[:, pidx][:, :, None, :]
    br, bi = bb_re.transpose(0, 2, 1)[:, None], bb_im.transpose(0, 2, 1)[:, None]
    ab_re = (pr * br - pi * bi).reshape(g, tt * hh, p)
    ab_im = (pr * bi + pi * br).reshape(g, tt * hh, p)
    w = jnp.concatenate([ab_re, ab_im, ab_im, ab_re], axis=-1)
    ridx = (tt - jnp.arange(tt)) if rev else (1 + jnp.arange(tt))
    pr = pg_re[:, ridx].transpose(0, 2, 1)[:, :, :, None]
    pi = pg_im[:, ridx].transpose(0, 2, 1)[:, :, :, None]
    cr, ci = c_re.transpose(0, 2, 1)[:, :, None, :], c_im.transpose(0, 2, 1)[:, :, None, :]
    v_re = (cr * pr - ci * pi).reshape(g, p, tt * hh)
    v_im = (cr * pi + ci * pr).reshape(g, p, tt * hh)
    v = jnp.concatenate([v_re, -v_im], axis=1)
    at_re, at_im = pw_re[tt], pw_im[tt]
    a3 = jnp.stack([jnp.concatenate([at_re, at_re], -1),
                    jnp.concatenate([-at_im, at_im], -1),
                    jnp.concatenate([at_im, -at_im], -1)], axis=1)
    a3 = jnp.pad(a3, ((0, 0), (0, 5), (0, 0)))
    return m.astype(BF16), w.astype(BF16), v.astype(BF16), a3


def _s5_kernel(z_ref, d_ref, perm_ref, mf_ref, mb_ref, wf_ref, wb_ref, vf_ref, vb_ref, af_ref, ab_ref,
               h0f_ref, h0fs_ref, h0b_ref, h0bs_ref, y_ref, finf_ref, finb_ref,
               up_s, sa_f, sb_f, sa_b, sb_b, hin_f, hin_b, *, nb, nc, gpb, hh, tt):
    r = nb * nc
    cw = tt * hh
    p2 = hin_f.shape[-1]
    nt = (((1,), (1,)), ((), ()))
    xcat = jnp.concatenate([z_ref[pl.ds(t, r, stride=tt), :] for t in range(tt)], axis=-1)
    up_s[...] = jnp.dot(xcat.astype(BF16), perm_ref[...], preferred_element_type=F32).astype(BF16)
    for g in range(gpb):
        ug = up_s[:, g * cw:(g + 1) * cw]
        s4 = jnp.dot(ug, wf_ref[g], preferred_element_type=F32)
        sa_f[g] = s4[:, :p2]
        sb_f[g] = s4[:, p2:]
        s4 = jnp.dot(ug, wb_ref[g], preferred_element_type=F32)
        sa_b[g] = s4[:, :p2]
        sb_b[g] = s4[:, p2:]

    def scan(a_ref, sa, sb, hin, h0_ref, h0s_ref, fin_ref, reverse):
        def body(i, carry):
            c = (nc - 1 - i) if reverse else i
            rows = pl.ds(c, nb, stride=nc)
            new = []
            for g in range(gpb):
                hcur, hswp = carry[2 * g], carry[2 * g + 1]
                hin[g, rows, :] = hcur
                ar = a_ref[g, 0:1, :]
                new.append(hcur * ar + hswp * a_ref[g, 1:2, :] + sa[g, rows, :])
                new.append(hswp * ar + hcur * a_ref[g, 2:3, :] + sb[g, rows, :])
            return tuple(new)

        init = []
        for g in range(gpb):
            init += [h0_ref[0, g], h0s_ref[0, g]]
        fin = lax.fori_loop(0, nc, body, tuple(init))
        for g in range(gpb):
            fin_ref[0, g] = fin[2 * g]

    scan(af_ref, sa_f, sb_f, hin_f, h0f_ref, h0fs_ref, finf_ref, False)
    scan(ab_ref, sa_b, sb_b, hin_b, h0b_ref, h0bs_ref, finb_ref, True)
    ys = []
    for g in range(gpb):
        ug = up_s[:, g * cw:(g + 1) * cw]
        yg = jnp.dot(ug, mf_ref[g], preferred_element_type=F32)
        yg = yg + jnp.dot(ug, mb_ref[g], preferred_element_type=F32)
        yg = yg + jnp.dot(hin_f[g].astype(BF16), vf_ref[g], preferred_element_type=F32)
        yg = yg + jnp.dot(hin_b[g].astype(BF16), vb_ref[g], preferred_element_type=F32)
        ys.append(yg)
    ycat = jnp.concatenate(ys, axis=-1)
    hi = ycat.astype(BF16)
    lo = (ycat - hi.astype(F32)).astype(BF16)
    ytok = (lax.dot_general(hi, perm_ref[...], nt, preferred_element_type=F32)
            + lax.dot_general(lo, perm_ref[...], nt, preferred_element_type=F32))
    lanes = z_ref.shape[1]
    for t in range(tt):
        rows = pl.ds(t, r, stride=tt)
        y_ref[rows, :] = ytok[:, t * lanes:(t + 1) * lanes] + z_ref[rows, :] * d_ref[...]


def _s5_call(z, col0, dvec, perm, ops_f, ops_b, h0, nbatch, seq):
    g, hh = dvec.shape
    mf, wf, vf, af = ops_f
    mb, wb, vb, ab = ops_b
    p2 = vf.shape[1]
    tt = S5_CHUNK
    cw = tt * hh
    gpb = LANES // hh
    nblk = g // gpb
    nsplit = 2 if nbatch % 2 == 0 else 1
    nb = nbatch // nsplit
    nc = seq // tt
    r = nb * nc
    tok = r * tt
    cb0 = col0 // LANES
    kern = functools.partial(_s5_kernel, nb=nb, nc=nc, gpb=gpb, hh=hh, tt=tt)
    gblk = lambda shape: pl.BlockSpec((gpb,) + shape, lambda j, s: (j, 0, 0))
    hblk = pl.BlockSpec((1, gpb, nb, p2), lambda j, s: (s, j, 0, 0))
    return pl.pallas_call(
        kern,
        grid=(nblk, nsplit),
        in_specs=[pl.BlockSpec((tok, LANES), lambda j, s: (s, cb0 + j)),
                  pl.BlockSpec((1, LANES), lambda j, s: (0, j)),
                  pl.BlockSpec(perm.shape, lambda j, s: (0, 0), pipeline_mode=pl.Buffered(1)),
                  gblk((cw, cw)), gblk((cw, cw)), gblk((cw, 2 * p2)), gblk((cw, 2 * p2)),
                  gblk((p2, cw)), gblk((p2, cw)), gblk((8, p2)), gblk((8, p2)),
                  hblk, hblk, hblk, hblk],
        out_specs=[pl.BlockSpec((tok, LANES), lambda j, s: (s, j)), hblk, hblk],
        out_shape=[jax.ShapeDtypeStruct((nbatch * seq, g * hh), F32),
                   jax.ShapeDtypeStruct((nsplit, g, nb, p2), F32),
                   jax.ShapeDtypeStruct((nsplit, g, nb, p2), F32)],
        scratch_shapes=[pltpu.VMEM((r, gpb * cw), BF16)]
        + [pltpu.VMEM((gpb, r, p2), F32) for _ in range(6)],
        compiler_params=_cparams("arbitrary", "arbitrary"),
        name="s5",
    )(z, dvec.reshape(1, g * hh), perm, mf, mb, wf, wb, vf, vb, af, ab, *h0)


def _s5_perm(hh, tt):
    gpb = LANES // hh
    src = jnp.arange(tt * LANES)
    t, ch = src // LANES, src % LANES
    dst = (ch // hh) * (tt * hh) + t * hh + ch % hh
    return (dst[:, None] == jnp.arange(gpb * tt * hh)[None, :]).astype(BF16)


def _s5_branch(z, col0, nbatch, seq, ops_f, ops_b, dvec, perm, h0_re, h0_im):
    g = dvec.shape[0]
    p = ops_f[2].shape[1] // 2
    nsplit = 2 if nbatch % 2 == 0 else 1
    nb = nbatch // nsplit

    def init(d):
        if h0_re is None:
            zero = jnp.zeros((nsplit, g, nb, 2 * p), F32)
            return zero, zero
        hr = h0_re[:, d].reshape(nsplit, nb, g, p).transpose(0, 2, 1, 3)
        hi = h0_im[:, d].reshape(nsplit, nb, g, p).transpose(0, 2, 1, 3)
        return jnp.concatenate([hr, hi], -1), jnp.concatenate([hi, hr], -1)

    y, finf, finb = _s5_call(z, col0, dvec, perm, ops_f, ops_b, init(0) + init(1), nbatch, seq)
    fin = jnp.stack([finf, finb], axis=0)
    fin = fin.transpose(1, 3, 0, 2, 4).reshape(nbatch, 2, g, 2 * p)
    return y, fin[..., :p], fin[..., p:]


def _hg_chunk(qr, fz, v, lb, st_ref, *, rev, nheads, dk):
    cc = qr.shape[0]
    q = qr * _sigmoid(qr)
    t = jnp.exp(-jnp.abs(fz))
    r = 1.0 / (1.0 + t)
    pos = fz >= 0.0
    sig = jnp.where(pos, r, t * r)
    nsig = jnp.where(pos, t * r, r)
    f = lb + (1.0 - lb) * sig
    k = (1.0 - lb) * nsig
    logf = jnp.log(f)
    ri = lax.broadcasted_iota(jnp.int32, (cc, cc), 0)
    ci = lax.broadcasted_iota(jnp.int32, (cc, cc), 1)
    tri = jnp.where((ci >= ri) if rev else (ci <= ri), 1.0, 0.0).astype(F32)
    cum = jnp.dot(tri, logf, precision=HIGHEST, preferred_element_type=F32)
    last = cum[0:1, :] if rev else cum[cc - 1:cc, :]
    ecum_q = (q * jnp.exp(cum)).astype(BF16)
    elast_k = (k * jnp.exp(last - cum)).astype(BF16)
    elast = jnp.exp(last)
    vb = v.astype(BF16)
    rowi = lax.broadcasted_iota(jnp.int32, (cc, 1), 0)

    levels = []
    b = cc // 2
    while b >= 1:
        target = b if rev else b - 1
        if b >= 4:
            pieces = [jnp.broadcast_to(cum[i * 2 * b + target:i * 2 * b + target + 1, :], (2 * b, cum.shape[1]))
                      for i in range(cc // (2 * b))]
            rb = pieces[0] if len(pieces) == 1 else jnp.concatenate(pieces, axis=0)
        else:
            pin = rowi % (2 * b)
            rb = cum
            for p in range(2 * b):
                delta = target - p
                if delta == 0:
                    continue
                shifted = pltpu.roll(cum, (-delta) % cc, axis=0)
                rb = jnp.where(pin == p, shifted, rb)
        x = jnp.exp(-jnp.abs(cum - rb))
        upper = ((rowi // b) % 2) == 1
        qmask = jnp.logical_not(upper) if rev else upper
        qx = jnp.where(qmask, q * x, 0.0).astype(BF16)
        kx = jnp.where(qmask, 0.0, k * x).astype(BF16)
        bm = None
        if 2 * b < cc:
            bm = jnp.where((ri // (2 * b)) == (ci // (2 * b)), 1.0, 0.0).astype(F32)
        levels.append((qx, kx, bm))
        b //= 2

    nt = (((1,), (1,)), ((), ()))
    outs = []
    for h in range(nheads):
        sl = slice(h * dk, (h + 1) * dk)
        att = None
        for qx, kx, bm in levels:
            a = lax.dot_general(qx[:, sl], kx[:, sl], nt, preferred_element_type=F32)
            if bm is not None:
                a = a * bm
            att = a if att is None else att + a
        dg = jnp.sum(q[:, sl] * k[:, sl], axis=-1, keepdims=True)
        st = st_ref[h]
        o = lax.dot_general(ecum_q[:, sl], st.astype(BF16), nt, preferred_element_type=F32)
        o = o + jnp.dot(att.astype(BF16), vb[:, sl], preferred_element_type=F32) + dg * v[:, sl]
        outs.append(o)
        vt = v[:, sl].T.astype(BF16)
        st_ref[h] = st * elast[:, sl] + jnp.dot(vt, elast_k[:, sl], preferred_element_type=F32)
    return outs


def _hg_load(ref, j, cc, gridded):
    if not gridded:
        return ref[0, j * cc:(j + 1) * cc, :]
    per = cc // ref.shape[1]
    return jnp.concatenate([ref[0, :, j * per + wi, :] for wi in range(per)], axis=0)


def _hg_store(ref, j, cc, gridded, val):
    if not gridded:
        ref[0, j * cc:(j + 1) * cc, :] = val
        return
    rows = ref.shape[1]
    for wi in range(cc // rows):
        ref[0, :, j * (cc // rows) + wi, :] = val[wi * rows:(wi + 1) * rows]


def _hgrn_kernel(*refs, rev, nheads, dk, cc, nsub, gridded, final):
    if final:
        q_ref, fz_ref, v_ref, lb_ref, s0_ref, of_ref, g_ref, ng_ref, o_ref, sfin_ref, st_ref = refs
    else:
        q_ref, fz_ref, v_ref, lb_ref, s0_ref, o_ref, sfin_ref, st_ref = refs
    c = pl.program_id(1)

    @pl.when(c == 0)
    def _():
        st_ref[...] = s0_ref[0]

    lb = lb_ref[...]
    for jj in range(nsub):
        j = (nsub - 1 - jj) if rev else jj
        outs = _hg_chunk(_hg_load(q_ref, j, cc, gridded), _hg_load(fz_ref, j, cc, gridded),
                         _hg_load(v_ref, j, cc, gridded), lb, st_ref, rev=rev, nheads=nheads, dk=dk)
        if final:
            of = _hg_load(of_ref, j, cc, gridded)
            g = _hg_load(g_ref, j, cc, gridded)
            normed = []
            for h in range(nheads):
                o = outs[h] + of[:, h * dk:(h + 1) * dk]
                ms = jnp.mean(o * o, axis=-1, keepdims=True)
                normed.append(o * lax.rsqrt(ms + LN_EPS))
            val = jnp.concatenate(normed, axis=-1) * ng_ref[...] * (g * _sigmoid(g))
        else:
            val = jnp.concatenate(outs, axis=-1)
        _hg_store(o_ref, j, cc, gridded, val)

    @pl.when(c == pl.num_programs(1) - 1)
    def _():
        sfin_ref[0] = st_ref[...]


def _hgrn_call(z, cols, fz_key, lb, s0t, nbatch, seq, rows, rev, o_fwd=None, ng=None):
    nheads, dv, dk = s0t.shape[1:]
    w = nheads * dk
    ncol = z.shape[1]
    cc = min(HG_CHUNK, seq)
    nsub = 2 if seq % (2 * cc) == 0 else 1
    blk = nsub * cc
    nc = seq // blk
    final = o_fwd is not None
    gridded = rows is not None
    cidx = (lambda c: nc - 1 - c) if rev else (lambda c: c)
    if gridded:
        wcols = blk // rows
        zv = z.reshape(nbatch, rows, GRID_W, ncol)
        spec = lambda cb: pl.BlockSpec((1, rows, wcols, w), lambda b, c: (b, 0, cidx(c), cb))
        ov = None if o_fwd is None else o_fwd.reshape(nbatch, rows, GRID_W, w)
        oshape = (nbatch, rows, GRID_W, w)
    else:
        zv = z.reshape(nbatch, seq, ncol)
        spec = lambda cb: pl.BlockSpec((1, blk, w), lambda b, c: (b, cidx(c), cb))
        ov = None if o_fwd is None else o_fwd.reshape(nbatch, seq, w)
        oshape = (nbatch, seq, w)
    sspec = pl.BlockSpec((1, nheads, dv, dk), lambda b, c: (b, 0, 0, 0))
    vspec = pl.BlockSpec((1, w), lambda b, c: (0, 0))
    in_specs = [spec(cols['q'] // w), spec(cols[fz_key] // w), spec(cols['v'] // w), vspec, sspec]
    args = [zv, zv, zv, lb, s0t]
    if final:
        in_specs += [spec(0), spec(cols['g'] // w), vspec]
        args += [ov, zv, ng]
    kern = functools.partial(_hgrn_kernel, rev=rev, nheads=nheads, dk=dk, cc=cc, nsub=nsub, gridded=gridded,
                             final=final)
    o, sfin = pl.pallas_call(
        kern,
        grid=(nbatch, nc),
        in_specs=in_specs,
        out_specs=[spec(0), sspec],
        out_shape=[jax.ShapeDtypeStruct(oshape, F32),
                   jax.ShapeDtypeStruct((nbatch, nheads, dv, dk), F32)],
        scratch_shapes=[pltpu.VMEM((nheads, dv, dk), F32)],
        compiler_params=_cparams("arbitrary", "arbitrary"),
        name="hgrn_bwd" if rev else "hgrn_fwd",
    )(*args)
    return o.reshape(nbatch * seq, w), sfin


def _merge1_kernel(y_ref, yb_ref, ga_ref, gb_ref, wglu_ref, bglu_ref, wa_ref, wb_ref, m_ref, ya_s, yb_s):
    @pl.when(pl.program_id(1) == 0)
    def _():
        y = jax.nn.gelu(y_ref[...])
        glu = jnp.dot(y.astype(BF16), wglu_ref[...], preferred_element_type=F32) + bglu_ref[...]
        ya_s[...] = (y * _sigmoid(glu)).astype(BF16)
        yb_s[...] = yb_ref[...].astype(BF16)

    pa = jnp.dot(ya_s[...], wa_ref[...], preferred_element_type=F32)
    pb = jnp.dot(yb_s[...], wb_ref[...], preferred_element_type=F32)
    m_ref[...] = (_sigmoid(ga_ref[...]) * pa + _sigmoid(gb_ref[...]) * pb).astype(BF16)


def _merge1_call(y_s5, y_hg, z, cols, wglu, bglu, wa, wb):
    t, sw = y_s5.shape
    hw = y_hg.shape[1]
    d = wa.shape[1]
    tm = _pick(t, 512, 8)
    tn = _pick(d, 1024)
    nj = d // tn
    ga0 = cols['ga'] // tn
    gb0 = cols['gb'] // tn
    return pl.pallas_call(
        _merge1_kernel,
        grid=(t // tm, nj),
        in_specs=[pl.BlockSpec((tm, sw), lambda i, j: (i, 0)),
                  pl.BlockSpec((tm, hw), lambda i, j: (i, 0)),
                  pl.BlockSpec((tm, tn), lambda i, j: (i, ga0 + j)),
                  pl.BlockSpec((tm, tn), lambda i, j: (i, gb0 + j)),
                  pl.BlockSpec((sw, sw), lambda i, j: (0, 0)),
                  pl.BlockSpec((1, sw), lambda i, j: (0, 0)),
                  pl.BlockSpec((sw, tn), lambda i, j: (0, j)),
                  pl.BlockSpec((hw, tn), lambda i, j: (0, j))],
        out_specs=pl.BlockSpec((tm, tn), lambda i, j: (i, j)),
        out_shape=jax.ShapeDtypeStruct((t, d), BF16),
        scratch_shapes=[pltpu.VMEM((tm, sw), BF16), pltpu.VMEM((tm, hw), BF16)],
        compiler_params=_cparams("arbitrary", "arbitrary"),
        name="merge1",
    )(y_s5, y_hg, z, z, wglu, bglu, wa, wb)


def _outproj_kernel(m_ref, w_ref, x_ref, g1_ref, t_ref, *, tiles_per_seq, mod_base, alpha):
    row = _mod_row(pl.program_id(0), tiles_per_seq, mod_base)
    mix = jnp.dot(m_ref[...], w_ref[...], preferred_element_type=F32)
    t_ref[...] = alpha * x_ref[...] + g1_ref[0, pl.ds(row, 1), :] * mix


def _outproj_call(m, w_out, x, mods, seq_len, mod_base, alpha):
    t, d = x.shape
    tm = _pick(seq_len if mod_base else t, 1024, 8)
    tn = _pick(d, 1024)
    tiles_per_seq = (seq_len // tm) if mod_base else (t // tm)
    kern = functools.partial(_outproj_kernel, tiles_per_seq=tiles_per_seq, mod_base=mod_base, alpha=alpha)
    return pl.pallas_call(
        kern,
        grid=(t // tm, d // tn),
        in_specs=[pl.BlockSpec((tm, d), lambda i, j: (i, 0)),
                  pl.BlockSpec((d, tn), lambda i, j: (0, j)),
                  pl.BlockSpec((tm, tn), lambda i, j: (i, j)),
                  pl.BlockSpec((1, mods.shape[1], tn), lambda i, j: (2, 0, j))],
        out_specs=pl.BlockSpec((tm, tn), lambda i, j: (i, j)),
        out_shape=jax.ShapeDtypeStruct((t, d), F32),
        compiler_params=_cparams("arbitrary", "arbitrary"),
        name="outproj",
    )(m, w_out, x, mods)


def _post_kernel(t_ref, mod_ref, lng_ref, lnb_ref, wr_ref, x1_ref, h2_ref, lg_ref, *, tiles_per_seq, mod_base):
    row = _mod_row(pl.program_id(0), tiles_per_seq, mod_base)
    sh2 = mod_ref[3, pl.ds(row, 1), :]
    sc2 = mod_ref[4, pl.ds(row, 1), :]
    x1 = _layer_norm(t_ref[...]) * lng_ref[...] + lnb_ref[...]
    x1_ref[...] = x1
    h2 = _layer_norm(x1) * (1.0 + sc2) + sh2
    h2_ref[...] = h2.astype(BF16)
    lg_ref[...] = jnp.dot(h2, wr_ref[...], precision=HIGHEST, preferred_element_type=F32)


def _post_call(tres, mods, lng, lnb, wr, seq_len, mod_base):
    t, d = tres.shape
    tm = _pick(seq_len if mod_base else t, 256, 8)
    tiles_per_seq = (seq_len // tm) if mod_base else (t // tm)
    kern = functools.partial(_post_kernel, tiles_per_seq=tiles_per_seq, mod_base=mod_base)
    return pl.pallas_call(
        kern,
        grid=(t // tm,),
        in_specs=[pl.BlockSpec((tm, d), lambda i: (i, 0)),
                  pl.BlockSpec(mods.shape, lambda i: (0, 0, 0)),
                  pl.BlockSpec((1, d), lambda i: (0, 0)),
                  pl.BlockSpec((1, d), lambda i: (0, 0)),
                  pl.BlockSpec((d, LANES), lambda i: (0, 0))],
        out_specs=[pl.BlockSpec((tm, d), lambda i: (i, 0)),
                   pl.BlockSpec((tm, d), lambda i: (i, 0)),
                   pl.BlockSpec((tm, LANES), lambda i: (i, 0))],
        out_shape=[jax.ShapeDtypeStruct((t, d), F32),
                   jax.ShapeDtypeStruct((t, d), BF16),
                   jax.ShapeDtypeStruct((t, LANES), F32)],
        compiler_params=_cparams("arbitrary"),
        name="post",
    )(tres, mods, lng, lnb, wr)


def _route_kernel(lg_ref, pos_ref, posc_ref, gatec_ref, aff_s, afft_s, cnt_s, *, ne, cap):
    n = lg_ref.shape[0]
    lg = lg_ref[...]
    lane = lax.broadcasted_iota(jnp.int32, (n, LANES), 1)
    lmask = jnp.where(lane < ne, lg, -jnp.inf)
    mx = jnp.max(lmask, axis=-1, keepdims=True)
    ex = jnp.exp(lmask - mx)
    aff = ex / jnp.sum(ex, axis=-1, keepdims=True)
    aff_s[...] = aff
    afft_s[...] = aff.T
    cnt_s[...] = jnp.zeros_like(cnt_s)
    sb = min(n, 64)
    tl = lax.broadcasted_iota(jnp.int32, (1, n), 1)

    def count(i, carry):
        s0 = pl.multiple_of(i * sb, sb)
        blk = aff_s[pl.ds(s0, sb), :]
        si = s0 + lax.broadcasted_iota(jnp.int32, (sb, 1), 0)
        tie = jnp.where(si < tl, 1.0, 0.0)
        for e in range(ne):
            col = blk[:, e:e + 1]
            row = afft_s[e:e + 1, :]
            beats = jnp.where(col > row, 1.0, jnp.where(col == row, tie, 0.0))
            cnt_s[e:e + 1, :] += jnp.sum(beats, axis=0, keepdims=True)
        return carry

    lax.fori_loop(0, n // sb, count, 0)
    sel = jnp.where(cnt_s[...] < cap, 1.0, 0.0)
    selb = sel.astype(BF16)
    pb = min(n, 256)
    ui = lax.broadcasted_iota(jnp.int32, (n, pb), 0)
    uj = lax.broadcasted_iota(jnp.int32, (n, pb), 1)
    pos_blocks = []
    for t0 in range(0, n, pb):
        upper = jnp.where(ui < uj + t0, 1.0, 0.0).astype(BF16)
        pos_blocks.append(jnp.dot(selb, upper, preferred_element_type=F32))
    pos = jnp.concatenate(pos_blocks, axis=-1) if len(pos_blocks) > 1 else pos_blocks[0]
    pos = jnp.where(sel > 0.0, pos, -1.0)
    gate = sel * afft_s[0:ne, :]
    pos_ref[0] = pos.astype(jnp.int32)
    pad = jnp.zeros((LANES - ne, n), F32)
    posc_ref[...] = jnp.concatenate([pos, pad], axis=0).T
    gatec_ref[...] = jnp.concatenate([gate, pad], axis=0).T


def _route_call(lg, nsets, n, ne, cap):
    kern = functools.partial(_route_kernel, ne=ne, cap=cap)
    return pl.pallas_call(
        kern,
        grid=(nsets,),
        in_specs=[pl.BlockSpec((n, LANES), lambda s: (s, 0))],
        out_specs=[pl.BlockSpec((1, ne, n), lambda s: (s, 0, 0)),
                   pl.BlockSpec((n, LANES), lambda s: (s, 0)),
                   pl.BlockSpec((n, LANES), lambda s: (s, 0))],
        out_shape=[jax.ShapeDtypeStruct((nsets, ne, n), jnp.int32),
                   jax.ShapeDtypeStruct((nsets * n, LANES), F32),
                   jax.ShapeDtypeStruct((nsets * n, LANES), F32)],
        scratch_shapes=[pltpu.VMEM((n, LANES), F32), pltpu.VMEM((LANES, n), F32), pltpu.VMEM((ne, n), F32)],
        compiler_params=_cparams("arbitrary"),
        name="route",
    )(lg)


def _gather_kernel(pos_ref, h_ref, x_ref, *, ne, cap):
    n = h_ref.shape[0]
    pos = pos_ref[0]
    h = h_ref[...]
    jidx = lax.broadcasted_iota(jnp.int32, (cap, n), 0)
    onehot = lambda e: jnp.where(pos[e:e + 1, :] == jidx, 1.0, 0.0).astype(BF16)
    if cap * ne <= 1024:
        p = jnp.concatenate([onehot(e) for e in range(ne)], axis=0)
        x = jnp.dot(p, h, preferred_element_type=F32).astype(BF16)
        for e in range(ne):
            x_ref[e] = x[e * cap:(e + 1) * cap]
    else:
        for e in range(ne):
            x_ref[e] = jnp.dot(onehot(e), h, preferred_element_type=F32).astype(BF16)


def _gather_call(pos, h2, nsets, n, ne, cap):
    d = h2.shape[1]
    td = _pick(d, 1024)
    kern = functools.partial(_gather_kernel, ne=ne, cap=cap)
    return pl.pallas_call(
        kern,
        grid=(nsets, d // td),
        in_specs=[pl.BlockSpec((1, ne, n), lambda s, j: (s, 0, 0)),
                  pl.BlockSpec((n, td), lambda s, j: (s, j))],
        out_specs=pl.BlockSpec((ne, cap, td), lambda s, j: (0, s, j)),
        out_shape=jax.ShapeDtypeStruct((ne, nsets * cap, d), BF16),
        compiler_params=_cparams("arbitrary", "arbitrary"),
        name="gather",
    )(pos, h2)


def _ffn_kernel(x_ref, wg_ref, wu_ref, wd_ref, y_ref, hid_ref, *, nf):
    j = pl.program_id(1)

    @pl.when(j < nf)
    def _():
        x = x_ref[0]
        a = jnp.dot(x, wg_ref[0].astype(BF16), preferred_element_type=F32)
        b = jnp.dot(x, wu_ref[0].astype(BF16), preferred_element_type=F32)
        hid = (a * _sigmoid(a) * b).astype(BF16)
        tf = hid.shape[1]
        for jj in range(nf):

            @pl.when(j == jj)
            def _():
                hid_ref[:, jj * tf:(jj + 1) * tf] = hid

    @pl.when(j >= nf)
    def _():
        y_ref[0] = jnp.dot(hid_ref[...], wd_ref[0].astype(BF16), preferred_element_type=F32).astype(BF16)


def _ffn_call(xg, w_gate, w_up, w_down):
    ne, r, d = xg.shape
    ff = w_gate.shape[2]
    tf = _pick(ff, 256)
    tn = _pick(d, 256)
    nf = ff // tf
    nn = d // tn
    kern = functools.partial(_ffn_kernel, nf=nf)
    return pl.pallas_call(
        kern,
        grid=(ne, nf + nn),
        in_specs=[pl.BlockSpec((1, r, d), lambda e, j: (e, 0, 0), pipeline_mode=pl.Buffered(1)),
                  pl.BlockSpec((1, d, tf), lambda e, j: (e, 0, jnp.minimum(j, nf - 1))),
                  pl.BlockSpec((1, d, tf), lambda e, j: (e, 0, jnp.minimum(j, nf - 1))),
                  pl.BlockSpec((1, ff, tn), lambda e, j: (e, 0, jnp.maximum(j - nf, 0)))],
        out_specs=pl.BlockSpec((1, r, tn), lambda e, j: (e, 0, jnp.maximum(j - nf, 0))),
        out_shape=jax.ShapeDtypeStruct((ne, r, d), BF16),
        scratch_shapes=[pltpu.VMEM((r, ff), BF16)],
        compiler_params=_cparams("arbitrary", "arbitrary"),
        name="ffn",
    )(xg, w_gate, w_up, w_down)


def _combine_kernel(posc_ref, gatec_ref, y_ref, x1_ref, mod_ref, lng_ref, lnb_ref, o_ref, acc_ref,
                    *, ke, cap, nk, tiles_per_seq, mod_base, alpha):
    k = pl.program_id(2)

    @pl.when(k == 0)
    def _():
        acc_ref[...] = jnp.zeros_like(acc_ref)

    lane = lax.broadcasted_iota(jnp.int32, (1, LANES), 1)
    posc = posc_ref[...]
    gatec = gatec_ref[...]

    def column(ee):
        m = lane == k * ke + ee
        return (jnp.sum(jnp.where(m, posc, 0.0), axis=-1, keepdims=True),
                jnp.sum(jnp.where(m, gatec, 0.0), axis=-1, keepdims=True))

    pieces = []
    if cap >= LANES:
        for ee in range(ke):
            pc, gc = column(ee)
            for c0 in range(0, cap, LANES):
                pieces.append(jnp.where(pc == (lane + c0).astype(F32), gc, 0.0))
    else:
        per = LANES // cap
        for e0 in range(0, ke, per):
            pb, gb = column(e0)
            for q in range(1, per):
                pq, gq = column(e0 + q)
                inq = lane >= q * cap
                pb = jnp.where(inq, pq, pb)
                gb = jnp.where(inq, gq, gb)
            pieces.append(jnp.where(pb == (lane % cap).astype(F32), gb, 0.0))
    gt = (jnp.concatenate(pieces, axis=-1) if len(pieces) > 1 else pieces[0]).astype(BF16)
    y = y_ref[...].reshape(ke * cap, y_ref.shape[-1])
    acc_ref[...] += jnp.dot(gt, y, preferred_element_type=F32)

    @pl.when(k == nk - 1)
    def _():
        row = _mod_row(pl.program_id(0) * pl.num_programs(1) + pl.program_id(1), tiles_per_seq, mod_base)
        g2 = mod_ref[5, pl.ds(row, 1), :]
        o_ref[...] = _layer_norm(alpha * x1_ref[...] + g2 * acc_ref[...]) * lng_ref[...] + lnb_ref[...]


def _combine_call(posc, gatec, y, x1, mods, lng, lnb, nsets, n, ne, cap, mod_base, alpha):
    t, d = x1.shape
    tq = _pick(n, 256, 8)
    nti = n // tq
    ke = max(1, min(ne, 512 // cap))
    nk = ne // ke
    tiles_per_seq = nti if mod_base else (t // tq)
    kern = functools.partial(_combine_kernel, ke=ke, cap=cap, nk=nk, tiles_per_seq=tiles_per_seq,
                             mod_base=mod_base, alpha=alpha)
    return pl.pallas_call(
        kern,
        grid=(nsets, nti, nk),
        in_specs=[pl.BlockSpec((tq, LANES), lambda s, i, k: (s * nti + i, 0)),
                  pl.BlockSpec((tq, LANES), lambda s, i, k: (s * nti + i, 0)),
                  pl.BlockSpec((ke, cap, d), lambda s, i, k: (k, s, 0)),
                  pl.BlockSpec((tq, d), lambda s, i, k: (s * nti + i, 0)),
                  pl.BlockSpec(mods.shape, lambda s, i, k: (0, 0, 0)),
                  pl.BlockSpec((1, d), lambda s, i, k: (0, 0)),
                  pl.BlockSpec((1, d), lambda s, i, k: (0, 0))],
        out_specs=pl.BlockSpec((tq, d), lambda s, i, k: (s * nti + i, 0)),
        out_shape=jax.ShapeDtypeStruct((t, d), F32),
        scratch_shapes=[pltpu.VMEM((tq, d), F32)],
        compiler_params=_cparams("arbitrary", "arbitrary", "arbitrary"),
        name="combine",
    )(posc, gatec, y, x1, mods, lng, lnb)


def _group_forward(x, nbatch, seq, mod_base, rows, mods, prm, s5_h0, hg_s0t, alpha):
    d = x.shape[-1]
    x2 = x.reshape(nbatch * seq, d)
    cols = prm['cols']
    z = _lnmm_call(x2, mods, prm['w_in'], seq, mod_base)

    y_s5, fin_re, fin_im = _s5_branch(z, cols['s5'], nbatch, seq, prm['s5_f'], prm['s5_b'], prm['s5_d'], prm['s5_perm'],
                                      None if s5_h0 is None else s5_h0[0], None if s5_h0 is None else s5_h0[1])

    o_f, sfin_f = _hgrn_call(z, cols, 'ff', prm['lb'][0], hg_s0t[:, 0], nbatch, seq, rows, False)
    y_hg, sfin_b = _hgrn_call(z, cols, 'fb', prm['lb'][1], hg_s0t[:, 1], nbatch, seq, rows, True,
                              o_fwd=o_f, ng=prm['hg_norm_g'])

    m = _merge1_call(y_s5, y_hg, z, cols, prm['w_glu'], prm['b_glu'], prm['w_proj_a'], prm['w_proj_b'])
    tres = _outproj_call(m, prm['w_out'], x2, mods, seq, mod_base, alpha)
    x1, h2, lg = _post_call(tres, mods, prm['ln1_g'], prm['ln1_b'], prm['w_router'], seq, mod_base)
    ne = prm['ne']
    cap = EC_FACTOR * seq // ne
    pos, posc, gatec = _route_call(lg, nbatch, seq, ne, cap)
    xg = _gather_call(pos, h2, nbatch, seq, ne, cap)
    yg = _ffn_call(xg, prm['w_gate'], prm['w_up'], prm['w_down'])
    out = _combine_call(posc, gatec, yg, x1, mods, prm['ln2_g'], prm['ln2_b'], nbatch, seq, ne, cap,
                        mod_base, alpha)
    hg_fin = jnp.stack([sfin_f, sfin_b], axis=1)
    return out.reshape(nbatch, seq, d), fin_re, fin_im, jnp.swapaxes(hg_fin, -1, -2)


def kernel(x_prompt, x_sample, state_s5_re, state_s5_im, state_hgrn, c, c_ctx, w_ada, b_ada, w_in, s5_lambda_re, s5_lambda_im, s5_log_step, s5_b_re, s5_b_im, s5_c_re, s5_c_im, s5_d, s5_w_glu, s5_b_glu, hg_lower_bounds, hg_norm_g, w_proj_a, w_proj_b, w_out, ln1_g, ln1_b, w_router, w_gate, w_up, w_down, ln2_g, ln2_b):
    depth = w_ada.shape[0]
    assert depth == 1, "single shared layer"
    nb_p, seq_p, d = x_prompt.shape
    nb_s, seq_s, _ = x_sample.shape
    g, p = s5_lambda_re.shape[2:]
    sw = s5_w_glu.shape[1]
    nheads, dk, dv = state_hgrn.shape[3:]
    hw = nheads * dk
    ne = w_gate.shape[1]
    alpha = (2 * depth) ** 0.25
    l = 0

    cond = jnp.concatenate([c_ctx[None], c, jnp.zeros((8 - 1 - nb_s, d), F32)], axis=0)
    mods = _ada_call(cond, w_ada[l], b_ada[l][None])
    mods = mods.reshape(8, 6, d).transpose(1, 0, 2)

    w_in_l = w_in[l]
    w_in_r = jnp.concatenate([w_in_l[:, sw + 5 * hw:], w_in_l[:, sw:sw + 5 * hw], w_in_l[:, :sw]], axis=1).astype(BF16)
    cols = {'ga': 0, 'gb': d, 'q': 2 * d, 'ff': 2 * d + hw, 'fb': 2 * d + 2 * hw, 'v': 2 * d + 3 * hw,
            'g': 2 * d + 4 * hw, 's5': 2 * d + 5 * hw}

    lb_all = jnp.cumsum(jax.nn.softmax(hg_lower_bounds.astype(F32), axis=1), axis=1)[:, l]
    ops = [_s5_operators(s5_lambda_re[l, dd], s5_lambda_im[l, dd], s5_log_step[l, dd], s5_b_re[l, dd],
                         s5_b_im[l, dd], s5_c_re[l, dd], s5_c_im[l, dd], dd == 1) for dd in range(2)]
    prm = {
        'cols': cols, 'hw': hw, 'nheads': nheads, 'ne': ne,
        'w_in': w_in_r, 's5_f': ops[0], 's5_b': ops[1], 's5_d': s5_d[l], 's5_perm': _s5_perm(s5_d.shape[-1], S5_CHUNK),
        'lb': lb_all[:, None, :],
        'w_glu': s5_w_glu[l].astype(BF16), 'b_glu': s5_b_glu[l][None], 'hg_norm_g': hg_norm_g[l][None],
        'w_proj_a': w_proj_a[l].astype(BF16), 'w_proj_b': w_proj_b[l].astype(BF16),
        'w_out': w_out[l].astype(BF16), 'ln1_g': ln1_g[l][None], 'ln1_b': ln1_b[l][None],
        'w_router': jnp.pad(w_router[l], ((0, 0), (0, LANES - ne))),
        'w_gate': w_gate[l], 'w_up': w_up[l], 'w_down': w_down[l],
        'ln2_g': ln2_g[l][None], 'ln2_b': ln2_b[l][None],
    }
    zero_hg = jnp.zeros((nb_p, 2, nheads, dv, dk), F32)
    yp, re_p, im_p, hg_p = _group_forward(x_prompt, nb_p, seq_p, 0, None, mods, prm, None, zero_hg, alpha)
    s0t = jnp.swapaxes(state_hgrn[:, l], -1, -2)
    ys, _, _, _ = _group_forward(x_sample, nb_s, seq_s, 1, seq_s // GRID_W, mods, prm,
                                 (state_s5_re[:, l], state_s5_im[:, l]), s0t, alpha)
    return yp, ys, re_p[:, None], im_p[:, None], hg_p[:, None]
```

```python
import functools
import math

import jax
import jax.numpy as jnp
from jax import lax
from jax.experimental import pallas as pl
from jax.experimental.pallas import tpu as pltpu

F32 = jnp.float32
BF16 = jnp.bfloat16
HIGHEST = lax.Precision.HIGHEST

GRID_W = 64
EC_FACTOR = 2
LN_EPS = 1e-6
S5_DT_MIN = 1e-3
S5_CHUNK = 16
HG_CHUNK = 128
LANES = 128
LOG2E = 1.4426950408889634
V7X_VMEM_LIMIT = 56 * 1024 * 1024


def _cparams(*sem):
    return pltpu.CompilerParams(dimension_semantics=sem, vmem_limit_bytes=V7X_VMEM_LIMIT)


def _pick(n, target, mult=LANES):
    if n <= target:
        return n
    t = (target // mult) * mult
    while t >= mult:
        if n % t == 0:
            return t
        t -= mult
    return n


def _layer_norm(x):
    mu = jnp.mean(x, axis=-1, keepdims=True)
    xc = x - mu
    var = jnp.mean(xc * xc, axis=-1, keepdims=True)
    return xc * lax.rsqrt(var + LN_EPS)


def _sigmoid(x):
    return jax.nn.sigmoid(x)


def _ada_kernel(c_ref, w_ref, b_ref, o_ref):
    c = c_ref[...]
    s = (c * _sigmoid(c)).astype(BF16)
    o_ref[...] = jnp.dot(s, w_ref[...].astype(BF16), preferred_element_type=F32) + b_ref[...]


def _ada_call(cond, w, b):
    m, d = cond.shape
    n = w.shape[1]
    tn = _pick(n, 512)
    return pl.pallas_call(
        _ada_kernel,
        grid=(n // tn,),
        in_specs=[pl.BlockSpec((m, d), lambda j: (0, 0)),
                  pl.BlockSpec((d, tn), lambda j: (0, j)),
                  pl.BlockSpec((1, tn), lambda j: (0, j))],
        out_specs=pl.BlockSpec((m, tn), lambda j: (0, j)),
        out_shape=jax.ShapeDtypeStruct((m, n), F32),
        compiler_params=_cparams("arbitrary"),
        name="ada",
    )(cond, w, b)


def _mod_row(i, tiles_per_seq, mod_base):
    return mod_base + i // tiles_per_seq


def _lnmm_kernel(x_ref, mod_ref, w_ref, o_ref, h_ref, *, tiles_per_seq, mod_base, nj):
    i = pl.program_id(0)
    j = pl.program_id(1)

    def prepare(tile, slot):
        row = _mod_row(tile, tiles_per_seq, mod_base)
        sh = mod_ref[0, pl.ds(row, 1), :]
        sc = mod_ref[1, pl.ds(row, 1), :]
        rb = math.gcd(x_ref.shape[0], LANES)

        def rows(c, carry):
            r0 = pl.multiple_of(c * rb, rb)
            xs = x_ref[pl.ds(r0, rb), :]
            h_ref[slot, pl.ds(r0, rb), :] = (_layer_norm(xs) * (1.0 + sc) + sh).astype(BF16)
            return carry

        lax.fori_loop(0, x_ref.shape[0] // rb, rows, 0)

    @pl.when((i == 0) & (j == 0))
    def _():
        prepare(0, 0)

    o_ref[...] = jnp.dot(h_ref[i % 2], w_ref[...], preferred_element_type=F32)

    @pl.when((j == nj - 1) & (i + 1 < pl.num_programs(0)))
    def _():
        prepare(i + 1, (i + 1) % 2)


def _lnmm_call(x, mods, w, seq_len, mod_base, col_shift):
    t, d = x.shape
    n = w.shape[1]
    tm = _pick(seq_len if mod_base else t, 512, 8)
    tn = _pick(math.gcd(n, col_shift), 1024)
    ni, nj = t // tm, n // tn
    shift = col_shift // tn
    tiles_per_seq = (seq_len // tm) if mod_base else ni
    kern = functools.partial(_lnmm_kernel, tiles_per_seq=tiles_per_seq, mod_base=mod_base, nj=nj)
    xmap = lambda i, j: (jnp.minimum(i + (j == nj - 1).astype(jnp.int32), ni - 1), 0)
    return pl.pallas_call(
        kern,
        grid=(ni, nj),
        in_specs=[pl.BlockSpec((tm, d), xmap),
                  pl.BlockSpec(mods.shape, lambda i, j: (0, 0, 0)),
                  pl.BlockSpec((d, tn), lambda i, j: (0, (j + shift) % nj))],
        out_specs=pl.BlockSpec((tm, tn), lambda i, j: (i, j)),
        out_shape=jax.ShapeDtypeStruct((t, n), F32),
        scratch_shapes=[pltpu.VMEM((2, tm, d), BF16)],
        compiler_params=_cparams("arbitrary", "arbitrary"),
        name="lnmm",
    )(x, mods, w)


def _s5_operators(lam_re, lam_im, log_step, b_re, b_im, c_re, c_im):
    nd, g, p = lam_re.shape
    hh = b_re.shape[-1]
    tt = S5_CHUNK
    g2 = nd * g
    lam_re, lam_im, log_step = lam_re.reshape(g2, p), lam_im.reshape(g2, p), log_step.reshape(g2)
    b_re, b_im = b_re.reshape(g2, p, hh), b_im.reshape(g2, p, hh)
    c_re, c_im = c_re.reshape(g2, hh, p), c_im.reshape(g2, hh, p)
    dt = jnp.exp(log_step)[:, None]
    mag = jnp.exp(lam_re * dt)
    abar_re = mag * jnp.cos(lam_im * dt)
    abar_im = mag * jnp.sin(lam_im * dt)
    nr, ni = abar_re - 1.0, abar_im
    den = lam_re * lam_re + lam_im * lam_im
    q_re = (nr * lam_re + ni * lam_im) / den
    q_im = (ni * lam_re - nr * lam_im) / den
    bb_re = q_re[..., None] * b_re - q_im[..., None] * b_im
    bb_im = q_re[..., None] * b_im + q_im[..., None] * b_re
    jj = jnp.arange(tt + 1, dtype=F32)[None, :, None]
    pmag = jnp.exp(jj * (lam_re * dt)[:, None, :])
    pg_re = pmag * jnp.cos(jj * (lam_im * dt)[:, None, :])
    pg_im = pmag * jnp.sin(jj * (lam_im * dt)[:, None, :])
    ca = jnp.concatenate([c_re[:, None] * pg_re[:, :tt, None, :] - c_im[:, None] * pg_im[:, :tt, None, :],
                          -(c_re[:, None] * pg_im[:, :tt, None, :] + c_im[:, None] * pg_re[:, :tt, None, :])],
                         axis=-1)
    bb = jnp.concatenate([bb_re, bb_im], axis=1)
    kk_all = jnp.einsum('gjhp,gpk->gjhk', ca, bb, precision=HIGHEST)
    out = []
    for dd in range(nd):
        rev = dd == 1
        gs = slice(dd * g, (dd + 1) * g)
        kk = kk_all[gs]
        s_idx = jnp.arange(tt)[:, None]
        t_idx = jnp.arange(tt)[None, :]
        lag = (s_idx - t_idx) if rev else (t_idx - s_idx)
        sel = (lag[None] == jnp.arange(tt)[:, None, None]).astype(F32)
        m = jnp.einsum('jst,gjhk->gskth', sel, kk, precision=HIGHEST).reshape(g, tt * hh, tt * hh)
        pidx = jnp.arange(tt) if rev else (tt - 1 - jnp.arange(tt))
        pr, pi = pg_re[gs][:, pidx][:, :, None, :], pg_im[gs][:, pidx][:, :, None, :]
        br, bi = bb_re[gs].transpose(0, 2, 1)[:, None], bb_im[gs].transpose(0, 2, 1)[:, None]
        ab_re = (pr * br - pi * bi).reshape(g, tt * hh, p)
        ab_im = (pr * bi + pi * br).reshape(g, tt * hh, p)
        w = jnp.concatenate([ab_re, ab_im, ab_im, ab_re], axis=-1)
        ridx = (tt - jnp.arange(tt)) if rev else (1 + jnp.arange(tt))
        pr = pg_re[gs][:, ridx].transpose(0, 2, 1)[:, :, :, None]
        pi = pg_im[gs][:, ridx].transpose(0, 2, 1)[:, :, :, None]
        cr, ci = c_re[gs].transpose(0, 2, 1)[:, :, None, :], c_im[gs].transpose(0, 2, 1)[:, :, None, :]
        v_re = (cr * pr - ci * pi).reshape(g, p, tt * hh)
        v_im = (cr * pi + ci * pr).reshape(g, p, tt * hh)
        v = jnp.concatenate([v_re, -v_im], axis=1)
        at_re, at_im = pg_re[gs][:, tt], pg_im[gs][:, tt]
        a3 = jnp.stack([jnp.concatenate([at_re, at_re], -1),
                        jnp.concatenate([-at_im, at_im], -1),
                        jnp.concatenate([at_im, -at_im], -1)], axis=1)
        a3 = jnp.pad(a3, ((0, 0), (0, 5), (0, 0)))
        out.append((m.astype(BF16), w.astype(BF16), v.astype(BF16), a3))
    return out


def _s5_kernel(z_ref, d_ref, perm_ref, mf_ref, mb_ref, wf_ref, wb_ref, vf_ref, vb_ref, af_ref, ab_ref,
               h0f_ref, h0fs_ref, h0b_ref, h0bs_ref, y_ref, finf_ref, finb_ref,
               up_s, sa_f, sb_f, sa_b, sb_b, hin_f, hin_b, *, nb, nc, gpb, hh, tt):
    r = nb * nc
    cw = tt * hh
    p2 = hin_f.shape[-1]
    nt = (((1,), (1,)), ((), ()))
    xcat = jnp.concatenate([z_ref[pl.ds(t, r, stride=tt), :] for t in range(tt)], axis=-1)
    up_s[...] = jnp.dot(xcat.astype(BF16), perm_ref[...], preferred_element_type=F32).astype(BF16)
    for g in range(gpb):
        ug = up_s[:, g * cw:(g + 1) * cw]
        s4 = jnp.dot(ug, wf_ref[g], preferred_element_type=F32)
        sa_f[g] = s4[:, :p2]
        sb_f[g] = s4[:, p2:]
        s4 = jnp.dot(ug, wb_ref[g], preferred_element_type=F32)
        sa_b[g] = s4[:, :p2]
        sb_b[g] = s4[:, p2:]

    def scan(a_ref, sa, sb, hin, h0_ref, h0s_ref, fin_ref, reverse):
        def body(i, carry):
            c = (nc - 1 - i) if reverse else i
            rows = pl.ds(c, nb, stride=nc)
            new = []
            for g in range(gpb):
                hcur, hswp = carry[2 * g], carry[2 * g + 1]
                hin[g, rows, :] = hcur
                ar = a_ref[g, 0:1, :]
                new.append(hcur * ar + hswp * a_ref[g, 1:2, :] + sa[g, rows, :])
                new.append(hswp * ar + hcur * a_ref[g, 2:3, :] + sb[g, rows, :])
            return tuple(new)

        init = []
        for g in range(gpb):
            init += [h0_ref[0, g], h0s_ref[0, g]]
        fin = lax.fori_loop(0, nc, body, tuple(init))
        for g in range(gpb):
            fin_ref[0, g] = fin[2 * g]

    scan(af_ref, sa_f, sb_f, hin_f, h0f_ref, h0fs_ref, finf_ref, False)
    scan(ab_ref, sa_b, sb_b, hin_b, h0b_ref, h0bs_ref, finb_ref, True)
    ys = []
    for g in range(gpb):
        ug = up_s[:, g * cw:(g + 1) * cw]
        yg = jnp.dot(ug, mf_ref[g], preferred_element_type=F32)
        yg = yg + jnp.dot(ug, mb_ref[g], preferred_element_type=F32)
        yg = yg + jnp.dot(hin_f[g].astype(BF16), vf_ref[g], preferred_element_type=F32)
        yg = yg + jnp.dot(hin_b[g].astype(BF16), vb_ref[g], preferred_element_type=F32)
        ys.append(yg)
    ycat = jnp.concatenate(ys, axis=-1)
    hi = ycat.astype(BF16)
    lo = (ycat - hi.astype(F32)).astype(BF16)
    ytok = (lax.dot_general(hi, perm_ref[...], nt, preferred_element_type=F32)
            + lax.dot_general(lo, perm_ref[...], nt, preferred_element_type=F32))
    lanes = z_ref.shape[1]
    for t in range(tt):
        rows = pl.ds(t, r, stride=tt)
        y_ref[rows, :] = ytok[:, t * lanes:(t + 1) * lanes] + z_ref[rows, :] * d_ref[...]


def _s5_call(z, col0, dvec, perm, ops_f, ops_b, h0, nbatch, seq):
    g, hh = dvec.shape
    mf, wf, vf, af = ops_f
    mb, wb, vb, ab = ops_b
    p2 = vf.shape[1]
    tt = S5_CHUNK
    cw = tt * hh
    gpb = LANES // hh
    nblk = g // gpb
    nsplit = 2 if nbatch % 2 == 0 else 1
    nb = nbatch // nsplit
    nc = seq // tt
    r = nb * nc
    tok = r * tt
    cb0 = col0 // LANES
    kern = functools.partial(_s5_kernel, nb=nb, nc=nc, gpb=gpb, hh=hh, tt=tt)
    gblk = lambda shape: pl.BlockSpec((gpb,) + shape, lambda j, s: (j, 0, 0))
    hblk = pl.BlockSpec((1, gpb, nb, p2), lambda j, s: (s, j, 0, 0))
    return pl.pallas_call(
        kern,
        grid=(nblk, nsplit),
        in_specs=[pl.BlockSpec((tok, LANES), lambda j, s: (s, cb0 + j)),
                  pl.BlockSpec((1, LANES), lambda j, s: (0, j)),
                  pl.BlockSpec(perm.shape, lambda j, s: (0, 0), pipeline_mode=pl.Buffered(1)),
                  gblk((cw, cw)), gblk((cw, cw)), gblk((cw, 2 * p2)), gblk((cw, 2 * p2)),
                  gblk((p2, cw)), gblk((p2, cw)), gblk((8, p2)), gblk((8, p2)),
                  hblk, hblk, hblk, hblk],
        out_specs=[pl.BlockSpec((tok, LANES), lambda j, s: (s, j)), hblk, hblk],
        out_shape=[jax.ShapeDtypeStruct((nbatch * seq, g * hh), F32),
                   jax.ShapeDtypeStruct((nsplit, g, nb, p2), F32),
                   jax.ShapeDtypeStruct((nsplit, g, nb, p2), F32)],
        scratch_shapes=[pltpu.VMEM((r, gpb * cw), BF16)]
        + [pltpu.VMEM((gpb, r, p2), F32) for _ in range(6)],
        compiler_params=_cparams("arbitrary", "arbitrary"),
        name="s5",
    )(z, dvec.reshape(1, g * hh), perm, mf, mb, wf, wb, vf, vb, af, ab, *h0)


def _s5_perm(hh, tt):
    gpb = LANES // hh
    src = jnp.arange(tt * LANES)
    t, ch = src // LANES, src % LANES
    dst = (ch // hh) * (tt * hh) + t * hh + ch % hh
    return (dst[:, None] == jnp.arange(gpb * tt * hh)[None, :]).astype(BF16)


def _s5_branch(z, col0, nbatch, seq, ops_f, ops_b, dvec, perm, h0_re, h0_im):
    g = dvec.shape[0]
    p = ops_f[2].shape[1] // 2
    nsplit = 2 if nbatch % 2 == 0 else 1
    nb = nbatch // nsplit

    def init(d):
        if h0_re is None:
            zero = jnp.zeros((nsplit, g, nb, 2 * p), F32)
            return zero, zero
        hr = h0_re[:, d].reshape(nsplit, nb, g, p).transpose(0, 2, 1, 3)
        hi = h0_im[:, d].reshape(nsplit, nb, g, p).transpose(0, 2, 1, 3)
        return jnp.concatenate([hr, hi], -1), jnp.concatenate([hi, hr], -1)

    y, finf, finb = _s5_call(z, col0, dvec, perm, ops_f, ops_b, init(0) + init(1), nbatch, seq)
    fin = jnp.stack([finf, finb], axis=0)
    fin = fin.transpose(1, 3, 0, 2, 4).reshape(nbatch, 2, g, 2 * p)
    return y, fin[..., :p], fin[..., p:]


def _hg_chunk(qr, fz, v, lb, st_ref, *, rev, nheads, dk):
    cc = qr.shape[0]
    q = qr * _sigmoid(qr)
    t = jnp.exp2(jnp.abs(fz) * (-LOG2E))
    r = 1.0 / (1.0 + t)
    pos = fz >= 0.0
    sig = jnp.where(pos, r, t * r)
    nsig = jnp.where(pos, t * r, r)
    f = lb + (1.0 - lb) * sig
    k = (1.0 - lb) * nsig
    logf = jnp.log(f)
    ri = lax.broadcasted_iota(jnp.int32, (cc, cc), 0)
    ci = lax.broadcasted_iota(jnp.int32, (cc, cc), 1)
    tri = jnp.where((ci >= ri) if rev else (ci <= ri), 1.0, 0.0).astype(BF16)
    l_hi = logf.astype(BF16)
    r1 = logf - l_hi.astype(F32)
    l_mid = r1.astype(BF16)
    l_lo = (r1 - l_mid.astype(F32)).astype(BF16)
    cum = (jnp.dot(tri, l_hi, preferred_element_type=F32) + jnp.dot(tri, l_mid, preferred_element_type=F32)
           + jnp.dot(tri, l_lo, preferred_element_type=F32))
    last = cum[0:1, :] if rev else cum[cc - 1:cc, :]
    ecum_q = (q * jnp.exp(cum)).astype(BF16)
    elast_k = (k * jnp.exp(last - cum)).astype(BF16)
    elast = jnp.exp(last)
    vb = v.astype(BF16)
    rowi = lax.broadcasted_iota(jnp.int32, (cc, 1), 0)

    levels = []
    b = cc // 2
    while b >= 1:
        target = b if rev else b - 1
        upper = ((rowi // b) % 2) == 1
        qrow = jnp.logical_not(upper) if rev else upper
        if b >= 4:
            pieces = [jnp.broadcast_to(cum[i * 2 * b + target:i * 2 * b + target + 1, :], (2 * b, cum.shape[1]))
                      for i in range(cc // (2 * b))]
            rb = pieces[0] if len(pieces) == 1 else jnp.concatenate(pieces, axis=0)
            x = jnp.exp2(jnp.abs(cum - rb) * (-LOG2E))
        elif b == 2:
            pin = rowi % 4
            f_next = pltpu.roll(f, cc - 1, axis=0)
            f_prev = pltpu.roll(f, 1, axis=0)
            if rev:
                x = jnp.where(pin == 0, f * f_next, jnp.where(pin == 1, f, jnp.where(pin == 2, 1.0, f_prev)))
            else:
                x = jnp.where(pin == 0, f_next, jnp.where(pin == 1, 1.0, jnp.where(pin == 2, f, f * f_prev)))
        else:
            x = jnp.where(qrow, f, 1.0)
        y = (jnp.where(qrow, q, k) * x).astype(BF16)
        qr_m = ((ri // b) % 2) == (0 if rev else 1)
        kc_m = ((ci // b) % 2) == (1 if rev else 0)
        bm = jnp.where(((ri // (2 * b)) == (ci // (2 * b))) & qr_m & kc_m, 1.0, 0.0).astype(F32)
        levels.append((y, bm))
        b //= 2

    nt = (((1,), (1,)), ((), ()))
    outs = []
    for h in range(nheads):
        sl = slice(h * dk, (h + 1) * dk)
        att = None
        for y, bm in levels:
            yh = y[:, sl]
            a = lax.dot_general(yh, yh, nt, preferred_element_type=F32) * bm
            att = a if att is None else att + a
        dg = jnp.sum(q[:, sl] * k[:, sl], axis=-1, keepdims=True)
        st = st_ref[h]
        o = lax.dot_general(ecum_q[:, sl], st.astype(BF16), nt, preferred_element_type=F32)
        o = o + jnp.dot(att.astype(BF16), vb[:, sl], preferred_element_type=F32) + dg * v[:, sl]
        outs.append(o)
        vt = v[:, sl].T.astype(BF16)
        st_ref[h] = st * elast[:, sl] + jnp.dot(vt, elast_k[:, sl], preferred_element_type=F32)
    return outs


def _hg_load(ref, j, cc, gridded):
    if not gridded:
        return ref[0, j * cc:(j + 1) * cc, :]
    per = cc // ref.shape[1]
    return jnp.concatenate([ref[0, :, j * per + wi, :] for wi in range(per)], axis=0)


def _hg_store(ref, j, cc, gridded, val):
    if not gridded:
        ref[0, j * cc:(j + 1) * cc, :] = val
        return
    rows = ref.shape[1]
    for wi in range(cc // rows):
        ref[0, :, j * (cc // rows) + wi, :] = val[wi * rows:(wi + 1) * rows]


def _hgrn_kernel(*refs, rev, nheads, dk, cc, nsub, gridded, final, zero_init):
    refs = list(refs)
    q_ref, fz_ref, v_ref, lb_ref = refs[:4]
    del refs[:4]
    s0_ref = None if zero_init else refs.pop(0)
    if final:
        of_ref, g_ref, ng_ref = refs[:3]
        del refs[:3]
    o_ref, sfin_ref, st_ref = refs
    c = pl.program_id(1)

    @pl.when(c == 0)
    def _():
        st_ref[...] = jnp.zeros_like(st_ref) if s0_ref is None else s0_ref[0]

    lb = lb_ref[...]
    for jj in range(nsub):
        j = (nsub - 1 - jj) if rev else jj
        outs = _hg_chunk(_hg_load(q_ref, j, cc, gridded), _hg_load(fz_ref, j, cc, gridded),
                         _hg_load(v_ref, j, cc, gridded), lb, st_ref, rev=rev, nheads=nheads, dk=dk)
        if final:
            of = _hg_load(of_ref, j, cc, gridded)
            g = _hg_load(g_ref, j, cc, gridded)
            normed = []
            for h in range(nheads):
                o = outs[h] + of[:, h * dk:(h + 1) * dk]
                ms = jnp.mean(o * o, axis=-1, keepdims=True)
                normed.append(o * lax.rsqrt(ms + LN_EPS))
            val = jnp.concatenate(normed, axis=-1) * ng_ref[...] * (g * _sigmoid(g))
        else:
            val = jnp.concatenate(outs, axis=-1)
        _hg_store(o_ref, j, cc, gridded, val)

    @pl.when(c == pl.num_programs(1) - 1)
    def _():
        sfin_ref[0] = st_ref[...]


def _hgrn_call(z, cols, fz_key, lb, s0t, state_shape, nbatch, seq, rows, rev, o_fwd=None, ng=None):
    nheads, dv, dk = state_shape
    w = nheads * dk
    ncol = z.shape[1]
    cc = min(HG_CHUNK, seq)
    nsub = 2 if seq % (2 * cc) == 0 else 1
    blk = nsub * cc
    nc = seq // blk
    final = o_fwd is not None
    gridded = rows is not None
    cidx = (lambda c: nc - 1 - c) if rev else (lambda c: c)
    if gridded:
        wcols = blk // rows
        zv = z.reshape(nbatch, rows, GRID_W, ncol)
        spec = lambda cb: pl.BlockSpec((1, rows, wcols, w), lambda b, c: (b, 0, cidx(c), cb))
        ov = None if o_fwd is None else o_fwd.reshape(nbatch, rows, GRID_W, w)
        oshape = (nbatch, rows, GRID_W, w)
    else:
        zv = z.reshape(nbatch, seq, ncol)
        spec = lambda cb: pl.BlockSpec((1, blk, w), lambda b, c: (b, cidx(c), cb))
        ov = None if o_fwd is None else o_fwd.reshape(nbatch, seq, w)
        oshape = (nbatch, seq, w)
    sspec = pl.BlockSpec((1, nheads, dv, dk), lambda b, c: (b, 0, 0, 0))
    vspec = pl.BlockSpec((1, w), lambda b, c: (0, 0))
    in_specs = [spec(cols['q'] // w), spec(cols[fz_key] // w), spec(cols['v'] // w), vspec]
    args = [zv, zv, zv, lb]
    if s0t is not None:
        in_specs.append(sspec)
        args.append(s0t)
    if final:
        in_specs += [spec(0), spec(cols['g'] // w), vspec]
        args += [ov, zv, ng]
    kern = functools.partial(_hgrn_kernel, rev=rev, nheads=nheads, dk=dk, cc=cc, nsub=nsub, gridded=gridded,
                             final=final, zero_init=s0t is None)
    o, sfin = pl.pallas_call(
        kern,
        grid=(nbatch, nc),
        in_specs=in_specs,
        out_specs=[spec(0), sspec],
        out_shape=[jax.ShapeDtypeStruct(oshape, F32),
                   jax.ShapeDtypeStruct((nbatch, nheads, dv, dk), F32)],
        scratch_shapes=[pltpu.VMEM((nheads, dv, dk), F32)],
        compiler_params=_cparams("arbitrary", "arbitrary"),
        name="hgrn_bwd" if rev else "hgrn_fwd",
    )(*args)
    return o.reshape(nbatch * seq, w), sfin


def _merge1_kernel(y_ref, yb_ref, ga_ref, gb_ref, wglu_ref, bglu_ref, wa_ref, wb_ref, m_ref, ya_s, yb_s):
    @pl.when(pl.program_id(1) == 0)
    def _():
        y = jax.nn.gelu(y_ref[...])
        glu = jnp.dot(y.astype(BF16), wglu_ref[...], preferred_element_type=F32) + bglu_ref[...]
        ya_s[...] = (y * _sigmoid(glu)).astype(BF16)
        yb_s[...] = yb_ref[...].astype(BF16)

    pa = jnp.dot(ya_s[...], wa_ref[...], preferred_element_type=F32)
    pb = jnp.dot(yb_s[...], wb_ref[...], preferred_element_type=F32)
    m_ref[...] = (_sigmoid(ga_ref[...]) * pa + _sigmoid(gb_ref[...]) * pb).astype(BF16)


def _merge1_call(y_s5, y_hg, z, cols, wglu, bglu, wa, wb):
    t, sw = y_s5.shape
    hw = y_hg.shape[1]
    d = wa.shape[1]
    tm = _pick(t, 512, 8)
    tn = _pick(math.gcd(d, cols['ga']), 1024)
    nj = d // tn
    ga0 = cols['ga'] // tn
    gb0 = cols['gb'] // tn
    return pl.pallas_call(
        _merge1_kernel,
        grid=(t // tm, nj),
        in_specs=[pl.BlockSpec((tm, sw), lambda i, j: (i, 0)),
                  pl.BlockSpec((tm, hw), lambda i, j: (i, 0)),
                  pl.BlockSpec((tm, tn), lambda i, j: (i, ga0 + j)),
                  pl.BlockSpec((tm, tn), lambda i, j: (i, gb0 + j)),
                  pl.BlockSpec((sw, sw), lambda i, j: (0, 0)),
                  pl.BlockSpec((1, sw), lambda i, j: (0, 0)),
                  pl.BlockSpec((sw, tn), lambda i, j: (0, j)),
                  pl.BlockSpec((hw, tn), lambda i, j: (0, j))],
        out_specs=pl.BlockSpec((tm, tn), lambda i, j: (i, j)),
        out_shape=jax.ShapeDtypeStruct((t, d), BF16),
        scratch_shapes=[pltpu.VMEM((tm, sw), BF16), pltpu.VMEM((tm, hw), BF16)],
        compiler_params=_cparams("arbitrary", "arbitrary"),
        name="merge1",
    )(y_s5, y_hg, z, z, wglu, bglu, wa, wb)


def _outproj_kernel(m_ref, w_ref, x_ref, g1_ref, t_ref, *, tiles_per_seq, mod_base, alpha):
    row = _mod_row(pl.program_id(0), tiles_per_seq, mod_base)
    mix = jnp.dot(m_ref[...], w_ref[...], preferred_element_type=F32)
    t_ref[...] = alpha * x_ref[...] + g1_ref[0, pl.ds(row, 1), :] * mix


def _outproj_call(m, w_out, x, mods, seq_len, mod_base, alpha):
    t, d = x.shape
    tm = _pick(seq_len if mod_base else t, 1024, 8)
    tn = _pick(d, 1024)
    tiles_per_seq = (seq_len // tm) if mod_base else (t // tm)
    kern = functools.partial(_outproj_kernel, tiles_per_seq=tiles_per_seq, mod_base=mod_base, alpha=alpha)
    return pl.pallas_call(
        kern,
        grid=(t // tm, d // tn),
        in_specs=[pl.BlockSpec((tm, d), lambda i, j: (i, 0)),
                  pl.BlockSpec((d, tn), lambda i, j: (0, j)),
                  pl.BlockSpec((tm, tn), lambda i, j: (i, j)),
                  pl.BlockSpec((1, mods.shape[1], tn), lambda i, j: (2, 0, j))],
        out_specs=pl.BlockSpec((tm, tn), lambda i, j: (i, j)),
        out_shape=jax.ShapeDtypeStruct((t, d), F32),
        compiler_params=_cparams("arbitrary", "arbitrary"),
        name="outproj",
    )(m, w_out, x, mods)


def _post_kernel(t_ref, mod_ref, lng_ref, lnb_ref, wr_ref, x1_ref, h2_ref, lg_ref, *, tiles_per_seq, mod_base):
    row = _mod_row(pl.program_id(0), tiles_per_seq, mod_base)
    sh2 = mod_ref[3, pl.ds(row, 1), :]
    sc2 = mod_ref[4, pl.ds(row, 1), :]
    x1 = _layer_norm(t_ref[...]) * lng_ref[...] + lnb_ref[...]
    x1_ref[...] = x1
    h2 = _layer_norm(x1) * (1.0 + sc2) + sh2
    h_hi = h2.astype(BF16)
    h2_ref[...] = h_hi
    h_mid = (h2 - h_hi.astype(F32)).astype(BF16)
    hw2 = jnp.dot(h_hi, wr_ref[...], preferred_element_type=F32)
    lg_ref[...] = (hw2[:, :LANES] + hw2[:, LANES:]
                   + jnp.dot(h_mid, wr_ref[:, :LANES], preferred_element_type=F32))


def _post_call(tres, mods, lng, lnb, wr, seq_len, mod_base):
    t, d = tres.shape
    tm = _pick(seq_len if mod_base else t, 256, 8)
    tiles_per_seq = (seq_len // tm) if mod_base else (t // tm)
    kern = functools.partial(_post_kernel, tiles_per_seq=tiles_per_seq, mod_base=mod_base)
    return pl.pallas_call(
        kern,
        grid=(t // tm,),
        in_specs=[pl.BlockSpec((tm, d), lambda i: (i, 0)),
                  pl.BlockSpec(mods.shape, lambda i: (0, 0, 0)),
                  pl.BlockSpec((1, d), lambda i: (0, 0)),
                  pl.BlockSpec((1, d), lambda i: (0, 0)),
                  pl.BlockSpec((d, 2 * LANES), lambda i: (0, 0))],
        out_specs=[pl.BlockSpec((tm, d), lambda i: (i, 0)),
                   pl.BlockSpec((tm, d), lambda i: (i, 0)),
                   pl.BlockSpec((tm, LANES), lambda i: (i, 0))],
        out_shape=[jax.ShapeDtypeStruct((t, d), F32),
                   jax.ShapeDtypeStruct((t, d), BF16),
                   jax.ShapeDtypeStruct((t, LANES), F32)],
        compiler_params=_cparams("arbitrary"),
        name="post",
    )(tres, mods, lng, lnb, wr)


def _route_kernel(lg_ref, pos_ref, posc_ref, gatec_ref, aff_s, afft_s, cnt_s, *, ne, cap):
    n = lg_ref.shape[0]
    lg = lg_ref[...]
    lane = lax.broadcasted_iota(jnp.int32, (n, LANES), 1)
    lmask = jnp.where(lane < ne, lg, -jnp.inf)
    mx = jnp.max(lmask, axis=-1, keepdims=True)
    ex = jnp.exp(lmask - mx)
    aff = ex / jnp.sum(ex, axis=-1, keepdims=True)
    aff_s[...] = aff
    afft_s[...] = aff.T
    cnt_s[...] = jnp.zeros_like(cnt_s)
    sb = min(n, 64)
    tl = lax.broadcasted_iota(jnp.int32, (1, n), 1)

    def count(i, carry):
        s0 = pl.multiple_of(i * sb, sb)
        blk = aff_s[pl.ds(s0, sb), :]
        si = s0 + lax.broadcasted_iota(jnp.int32, (sb, 1), 0)
        tie = jnp.where(si < tl, 1.0, 0.0)
        for e in range(ne):
            col = blk[:, e:e + 1]
            row = afft_s[e:e + 1, :]
            beats = jnp.where(col > row, 1.0, jnp.where(col == row, tie, 0.0))
            cnt_s[e:e + 1, :] += jnp.sum(beats, axis=0, keepdims=True)
        return carry

    lax.fori_loop(0, n // sb, count, 0)
    sel = jnp.where(cnt_s[...] < cap, 1.0, 0.0)
    selb = sel.astype(BF16)
    pb = min(n, 256)
    ui = lax.broadcasted_iota(jnp.int32, (n, pb), 0)
    uj = lax.broadcasted_iota(jnp.int32, (n, pb), 1)
    pos_blocks = []
    for t0 in range(0, n, pb):
        upper = jnp.where(ui < uj + t0, 1.0, 0.0).astype(BF16)
        pos_blocks.append(jnp.dot(selb, upper, preferred_element_type=F32))
    pos = jnp.concatenate(pos_blocks, axis=-1) if len(pos_blocks) > 1 else pos_blocks[0]
    pos = jnp.where(sel > 0.0, pos, -1.0)
    gate = sel * afft_s[0:ne, :]
    pos_ref[0] = pos.astype(jnp.int32)
    pad = jnp.zeros((LANES - ne, n), F32)
    posc_ref[...] = jnp.concatenate([pos, pad], axis=0).T
    gatec_ref[...] = jnp.concatenate([gate, pad], axis=0).T


def _route_call(lg, nsets, n, ne, cap):
    kern = functools.partial(_route_kernel, ne=ne, cap=cap)
    return pl.pallas_call(
        kern,
        grid=(nsets,),
        in_specs=[pl.BlockSpec((n, LANES), lambda s: (s, 0))],
        out_specs=[pl.BlockSpec((1, ne, n), lambda s: (s, 0, 0)),
                   pl.BlockSpec((n, LANES), lambda s: (s, 0)),
                   pl.BlockSpec((n, LANES), lambda s: (s, 0))],
        out_shape=[jax.ShapeDtypeStruct((nsets, ne, n), jnp.int32),
                   jax.ShapeDtypeStruct((nsets * n, LANES), F32),
                   jax.ShapeDtypeStruct((nsets * n, LANES), F32)],
        scratch_shapes=[pltpu.VMEM((n, LANES), F32), pltpu.VMEM((LANES, n), F32), pltpu.VMEM((ne, n), F32)],
        compiler_params=_cparams("arbitrary"),
        name="route",
    )(lg)


def _gather_kernel(pos_ref, h_ref, x_ref, *, ne, cap):
    n = h_ref.shape[0]
    pos = pos_ref[0]
    h = h_ref[...]
    jidx = lax.broadcasted_iota(jnp.int32, (cap, n), 0)
    onehot = lambda e: jnp.where(pos[e:e + 1, :] == jidx, 1.0, 0.0).astype(BF16)
    if cap * ne <= 1024:
        p = jnp.concatenate([onehot(e) for e in range(ne)], axis=0)
        x = jnp.dot(p, h, preferred_element_type=F32).astype(BF16)
        for e in range(ne):
            x_ref[e] = x[e * cap:(e + 1) * cap]
    else:
        for e in range(ne):
            x_ref[e] = jnp.dot(onehot(e), h, preferred_element_type=F32).astype(BF16)


def _gather_call(pos, h2, nsets, n, ne, cap):
    d = h2.shape[1]
    td = _pick(d, 1024)
    kern = functools.partial(_gather_kernel, ne=ne, cap=cap)
    return pl.pallas_call(
        kern,
        grid=(nsets, d // td),
        in_specs=[pl.BlockSpec((1, ne, n), lambda s, j: (s, 0, 0)),
                  pl.BlockSpec((n, td), lambda s, j: (s, j))],
        out_specs=pl.BlockSpec((ne, cap, td), lambda s, j: (0, s, j)),
        out_shape=jax.ShapeDtypeStruct((ne, nsets * cap, d), BF16),
        compiler_params=_cparams("arbitrary", "arbitrary"),
        name="gather",
    )(pos, h2)


def _ffn_kernel(x_ref, wg_ref, wu_ref, wd_ref, y_ref, hid_ref, *, nf):
    j = pl.program_id(1)

    @pl.when(j < nf)
    def _():
        x = x_ref[0]
        a = jnp.dot(x, wg_ref[0].astype(BF16), preferred_element_type=F32)
        b = jnp.dot(x, wu_ref[0].astype(BF16), preferred_element_type=F32)
        hid = (a * _sigmoid(a) * b).astype(BF16)
        tf = hid.shape[1]
        for jj in range(nf):

            @pl.when(j == jj)
            def _():
                hid_ref[:, jj * tf:(jj + 1) * tf] = hid

    @pl.when(j >= nf)
    def _():
        y_ref[0] = jnp.dot(hid_ref[...], wd_ref[0].astype(BF16), preferred_element_type=F32).astype(BF16)


def _ffn_call(xg, w_gate, w_up, w_down):
    ne, r, d = xg.shape
    ff = w_gate.shape[2]
    tf = _pick(ff, 256)
    tn = _pick(d, 256)
    nf = ff // tf
    nn = d // tn
    kern = functools.partial(_ffn_kernel, nf=nf)
    return pl.pallas_call(
        kern,
        grid=(ne, nf + nn),
        in_specs=[pl.BlockSpec((1, r, d), lambda e, j: (e, 0, 0), pipeline_mode=pl.Buffered(1)),
                  pl.BlockSpec((1, d, tf), lambda e, j: (e, 0, jnp.minimum(j, nf - 1))),
                  pl.BlockSpec((1, d, tf), lambda e, j: (e, 0, jnp.minimum(j, nf - 1))),
                  pl.BlockSpec((1, ff, tn), lambda e, j: (e, 0, jnp.maximum(j - nf, 0)))],
        out_specs=pl.BlockSpec((1, r, tn), lambda e, j: (e, 0, jnp.maximum(j - nf, 0))),
        out_shape=jax.ShapeDtypeStruct((ne, r, d), BF16),
        scratch_shapes=[pltpu.VMEM((r, ff), BF16)],
        compiler_params=_cparams("arbitrary", "arbitrary"),
        name="ffn",
    )(xg, w_gate, w_up, w_down)


def _combine_kernel(posc_ref, gatec_ref, y_ref, x1_ref, mod_ref, lng_ref, lnb_ref, o_ref, acc_ref,
                    *, ke, cap, nk, tiles_per_seq, mod_base, alpha):
    k = pl.program_id(2)

    @pl.when(k == 0)
    def _():
        acc_ref[...] = jnp.zeros_like(acc_ref)

    lane = lax.broadcasted_iota(jnp.int32, (1, LANES), 1)
    posc = posc_ref[...]
    gatec = gatec_ref[...]

    def column(ee):
        m = lane == k * ke + ee
        return (jnp.sum(jnp.where(m, posc, 0.0), axis=-1, keepdims=True),
                jnp.sum(jnp.where(m, gatec, 0.0), axis=-1, keepdims=True))

    pieces = []
    if cap >= LANES:
        for ee in range(ke):
            pc, gc = column(ee)
            for c0 in range(0, cap, LANES):
                pieces.append(jnp.where(pc == (lane + c0).astype(F32), gc, 0.0))
    else:
        per = LANES // cap
        for e0 in range(0, ke, per):
            pb, gb = column(e0)
            for q in range(1, per):
                pq, gq = column(e0 + q)
                inq = lane >= q * cap
                pb = jnp.where(inq, pq, pb)
                gb = jnp.where(inq, gq, gb)
            pieces.append(jnp.where(pb == (lane % cap).astype(F32), gb, 0.0))
    gt = (jnp.concatenate(pieces, axis=-1) if len(pieces) > 1 else pieces[0]).astype(BF16)
    y = y_ref[...].reshape(ke * cap, y_ref.shape[-1])
    acc_ref[...] += jnp.dot(gt, y, preferred_element_type=F32)

    @pl.when(k == nk - 1)
    def _():
        row = _mod_row(pl.program_id(0) * pl.num_programs(1) + pl.program_id(1), tiles_per_seq, mod_base)
        g2 = mod_ref[5, pl.ds(row, 1), :]
        o_ref[...] = _layer_norm(alpha * x1_ref[...] + g2 * acc_ref[...]) * lng_ref[...] + lnb_ref[...]


def _combine_call(posc, gatec, y, x1, mods, lng, lnb, nsets, n, ne, cap, mod_base, alpha):
    t, d = x1.shape
    tq = _pick(n, 256, 8)
    nti = n // tq
    ke = max(1, min(ne, 1024 // cap))
    nk = ne // ke
    tiles_per_seq = nti if mod_base else (t // tq)
    kern = functools.partial(_combine_kernel, ke=ke, cap=cap, nk=nk, tiles_per_seq=tiles_per_seq,
                             mod_base=mod_base, alpha=alpha)
    return pl.pallas_call(
        kern,
        grid=(nsets, nti, nk),
        in_specs=[pl.BlockSpec((tq, LANES), lambda s, i, k: (s * nti + i, 0)),
                  pl.BlockSpec((tq, LANES), lambda s, i, k: (s * nti + i, 0)),
                  pl.BlockSpec((ke, cap, d), lambda s, i, k: (k, s, 0)),
                  pl.BlockSpec((tq, d), lambda s, i, k: (s * nti + i, 0)),
                  pl.BlockSpec(mods.shape, lambda s, i, k: (0, 0, 0)),
                  pl.BlockSpec((1, d), lambda s, i, k: (0, 0)),
                  pl.BlockSpec((1, d), lambda s, i, k: (0, 0))],
        out_specs=pl.BlockSpec((tq, d), lambda s, i, k: (s * nti + i, 0)),
        out_shape=jax.ShapeDtypeStruct((t, d), F32),
        scratch_shapes=[pltpu.VMEM((tq, d), F32)],
        compiler_params=_cparams("arbitrary", "arbitrary", "arbitrary"),
        name="combine",
    )(posc, gatec, y, x1, mods, lng, lnb)


def _group_forward(x, nbatch, seq, mod_base, rows, mods, prm, s5_h0, hg_s0t, alpha):
    d = x.shape[-1]
    x2 = x.reshape(nbatch * seq, d)
    cols = prm['cols']
    z = _lnmm_call(x2, mods, prm['w_in'], seq, mod_base, prm['col_shift'])

    y_s5, fin_re, fin_im = _s5_branch(z, cols['s5'], nbatch, seq, prm['s5_f'], prm['s5_b'], prm['s5_d'], prm['s5_perm'],
                                      None if s5_h0 is None else s5_h0[0], None if s5_h0 is None else s5_h0[1])

    s0f, s0b = (None, None) if hg_s0t is None else (hg_s0t[:, 0], hg_s0t[:, 1])
    o_f, sfin_f = _hgrn_call(z, cols, 'ff', prm['lb'][0], s0f, prm['hg_state'], nbatch, seq, rows, False)
    y_hg, sfin_b = _hgrn_call(z, cols, 'fb', prm['lb'][1], s0b, prm['hg_state'], nbatch, seq, rows, True,
                              o_fwd=o_f, ng=prm['hg_norm_g'])

    m = _merge1_call(y_s5, y_hg, z, cols, prm['w_glu'], prm['b_glu'], prm['w_proj_a'], prm['w_proj_b'])
    tres = _outproj_call(m, prm['w_out'], x2, mods, seq, mod_base, alpha)
    x1, h2, lg = _post_call(tres, mods, prm['ln1_g'], prm['ln1_b'], prm['w_router'], seq, mod_base)
    ne = prm['ne']
    cap = EC_FACTOR * seq // ne
    pos, posc, gatec = _route_call(lg, nbatch, seq, ne, cap)
    xg = _gather_call(pos, h2, nbatch, seq, ne, cap)
    yg = _ffn_call(xg, prm['w_gate'], prm['w_up'], prm['w_down'])
    out = _combine_call(posc, gatec, yg, x1, mods, prm['ln2_g'], prm['ln2_b'], nbatch, seq, ne, cap,
                        mod_base, alpha)
    hg_fin = jnp.stack([sfin_f, sfin_b], axis=1)
    return out.reshape(nbatch, seq, d), fin_re, fin_im, jnp.swapaxes(hg_fin, -1, -2)


def kernel(x_prompt, x_sample, state_s5_re, state_s5_im, state_hgrn, c, c_ctx, w_ada, b_ada, w_in, s5_lambda_re, s5_lambda_im, s5_log_step, s5_b_re, s5_b_im, s5_c_re, s5_c_im, s5_d, s5_w_glu, s5_b_glu, hg_lower_bounds, hg_norm_g, w_proj_a, w_proj_b, w_out, ln1_g, ln1_b, w_router, w_gate, w_up, w_down, ln2_g, ln2_b):
    depth = w_ada.shape[0]
    assert depth == 1, "single shared layer"
    nb_p, seq_p, d = x_prompt.shape
    nb_s, seq_s, _ = x_sample.shape
    g, p = s5_lambda_re.shape[2:]
    sw = s5_w_glu.shape[1]
    nheads, dk, dv = state_hgrn.shape[3:]
    hw = nheads * dk
    ne = w_gate.shape[1]
    alpha = (2 * depth) ** 0.25
    l = 0

    cond = jnp.concatenate([c_ctx[None], c, jnp.zeros((8 - 1 - nb_s, d), F32)], axis=0)
    mods = _ada_call(cond, w_ada[l], b_ada[l][None])
    mods = mods.reshape(8, 6, d).transpose(1, 0, 2)

    w_in_b = w_in[l].astype(BF16)
    cols = {'q': 0, 'ff': hw, 'fb': 2 * hw, 'v': 3 * hw, 'g': 4 * hw, 'ga': 5 * hw, 'gb': 5 * hw + d,
            's5': 5 * hw + 2 * d}

    lb_all = jnp.cumsum(jax.nn.softmax(hg_lower_bounds.astype(F32), axis=1), axis=1)[:, l]
    ops = _s5_operators(s5_lambda_re[l], s5_lambda_im[l], s5_log_step[l], s5_b_re[l], s5_b_im[l],
                        s5_c_re[l], s5_c_im[l])
    wr = jnp.pad(w_router[l], ((0, 0), (0, LANES - ne)))
    wr_hi = wr.astype(BF16)
    wr_split = jnp.concatenate([wr_hi, (wr - wr_hi.astype(F32)).astype(BF16)], axis=1)
    prm = {
        'cols': cols, 'ne': ne, 'hg_state': (nheads, dv, dk), 'col_shift': sw,
        'w_in': w_in_b, 's5_f': ops[0], 's5_b': ops[1], 's5_d': s5_d[l], 's5_perm': _s5_perm(s5_d.shape[-1], S5_CHUNK),
        'lb': lb_all[:, None, :],
        'w_glu': s5_w_glu[l].astype(BF16), 'b_glu': s5_b_glu[l][None], 'hg_norm_g': hg_norm_g[l][None],
        'w_proj_a': w_proj_a[l].astype(BF16), 'w_proj_b': w_proj_b[l].astype(BF16),
        'w_out': w_out[l].astype(BF16), 'ln1_g': ln1_g[l][None], 'ln1_b': ln1_b[l][None],
        'w_router': wr_split,
        'w_gate': w_gate[l], 'w_up': w_up[l], 'w_down': w_down[l],
        'ln2_g': ln2_g[l][None], 'ln2_b': ln2_b[l][None],
    }
    yp, re_p, im_p, hg_p = _group_forward(x_prompt, nb_p, seq_p, 0, None, mods, prm, None, None, alpha)
    s0t = jnp.swapaxes(state_hgrn[:, l], -1, -2)
    ys, _, _, _ = _group_forward(x_sample, nb_s, seq_s, 1, seq_s // GRID_W, mods, prm,
                                 (state_s5_re[:, l], state_s5_im[:, l]), s0t, alpha)
    return yp, ys, re_p[:, None], im_p[:, None], hg_p[:, None]
```

```python
import functools
import math

import jax
import jax.numpy as jnp
from jax import lax
from jax.experimental import pallas as pl
from jax.experimental.pallas import tpu as pltpu

F32 = jnp.float32
BF16 = jnp.bfloat16
HIGHEST = lax.Precision.HIGHEST

GRID_W = 64
EC_FACTOR = 2
LN_EPS = 1e-6
S5_DT_MIN = 1e-3
S5_CHUNK = 16
HG_CHUNK = 128
LANES = 128
LOG2E = 1.4426950408889634
V7X_VMEM_LIMIT = 56 * 1024 * 1024


def _cparams(*sem):
    return pltpu.CompilerParams(dimension_semantics=sem, vmem_limit_bytes=V7X_VMEM_LIMIT)


def _pick(n, target, mult=LANES):
    if n <= target:
        return n
    t = (target // mult) * mult
    while t >= mult:
        if n % t == 0:
            return t
        t -= mult
    return n


def _layer_norm(x):
    mu = jnp.mean(x, axis=-1, keepdims=True)
    xc = x - mu
    var = jnp.mean(xc * xc, axis=-1, keepdims=True)
    return xc * lax.rsqrt(var + LN_EPS)


def _sigmoid(x):
    return jax.nn.sigmoid(x)


def _ada_kernel(c_ref, w_ref, b_ref, o_ref):
    c = c_ref[...]
    s = (c * _sigmoid(c)).astype(BF16)
    o_ref[...] = jnp.dot(s, w_ref[...].astype(BF16), preferred_element_type=F32) + b_ref[...]


def _ada_call(cond, w, b):
    m, d = cond.shape
    n = w.shape[1]
    tn = _pick(n, 512)
    return pl.pallas_call(
        _ada_kernel,
        grid=(n // tn,),
        in_specs=[pl.BlockSpec((m, d), lambda j: (0, 0)),
                  pl.BlockSpec((d, tn), lambda j: (0, j)),
                  pl.BlockSpec((1, tn), lambda j: (0, j))],
        out_specs=pl.BlockSpec((m, tn), lambda j: (0, j)),
        out_shape=jax.ShapeDtypeStruct((m, n), F32),
        compiler_params=_cparams("arbitrary"),
        name="ada",
    )(cond, w, b)


def _mod_row(i, tiles_per_seq, mod_base):
    return mod_base + i // tiles_per_seq


def _lnmm_kernel(x_ref, mod_ref, w_ref, o_ref, h_ref, *, tiles_per_seq, mod_base, nj):
    i = pl.program_id(0)
    j = pl.program_id(1)

    def prepare(tile, slot):
        row = _mod_row(tile, tiles_per_seq, mod_base)
        sh = mod_ref[0, pl.ds(row, 1), :]
        sc = mod_ref[1, pl.ds(row, 1), :]
        rb = math.gcd(x_ref.shape[0], LANES)

        def rows(c, carry):
            r0 = pl.multiple_of(c * rb, rb)
            xs = x_ref[pl.ds(r0, rb), :]
            h_ref[slot, pl.ds(r0, rb), :] = (_layer_norm(xs) * (1.0 + sc) + sh).astype(BF16)
            return carry

        lax.fori_loop(0, x_ref.shape[0] // rb, rows, 0)

    @pl.when((i == 0) & (j == 0))
    def _():
        prepare(0, 0)

    o_ref[...] = jnp.dot(h_ref[i % 2], w_ref[...], preferred_element_type=F32)

    @pl.when((j == nj - 1) & (i + 1 < pl.num_programs(0)))
    def _():
        prepare(i + 1, (i + 1) % 2)


def _lnmm_call(x, mods, w, seq_len, mod_base, col_shift):
    t, d = x.shape
    n = w.shape[1]
    tm = _pick(seq_len if mod_base else t, 512, 8)
    tn = _pick(math.gcd(n, col_shift), 1024)
    ni, nj = t // tm, n // tn
    shift = col_shift // tn
    tiles_per_seq = (seq_len // tm) if mod_base else ni
    kern = functools.partial(_lnmm_kernel, tiles_per_seq=tiles_per_seq, mod_base=mod_base, nj=nj)
    xmap = lambda i, j: (jnp.minimum(i + (j == nj - 1).astype(jnp.int32), ni - 1), 0)
    return pl.pallas_call(
        kern,
        grid=(ni, nj),
        in_specs=[pl.BlockSpec((tm, d), xmap),
                  pl.BlockSpec(mods.shape, lambda i, j: (0, 0, 0)),
                  pl.BlockSpec((d, tn), lambda i, j: (0, (j + shift) % nj))],
        out_specs=pl.BlockSpec((tm, tn), lambda i, j: (i, j)),
        out_shape=jax.ShapeDtypeStruct((t, n), F32),
        scratch_shapes=[pltpu.VMEM((2, tm, d), BF16)],
        compiler_params=_cparams("arbitrary", "arbitrary"),
        name="lnmm",
    )(x, mods, w)


def _s5_operators(lam_re, lam_im, log_step, b_re, b_im, c_re, c_im):
    nd, g, p = lam_re.shape
    hh = b_re.shape[-1]
    tt = S5_CHUNK
    g2 = nd * g
    lam_re, lam_im, log_step = lam_re.reshape(g2, p), lam_im.reshape(g2, p), log_step.reshape(g2)
    b_re, b_im = b_re.reshape(g2, p, hh), b_im.reshape(g2, p, hh)
    c_re, c_im = c_re.reshape(g2, hh, p), c_im.reshape(g2, hh, p)
    dt = jnp.exp(log_step)[:, None]
    mag = jnp.exp(lam_re * dt)
    abar_re = mag * jnp.cos(lam_im * dt)
    abar_im = mag * jnp.sin(lam_im * dt)
    nr, ni = abar_re - 1.0, abar_im
    den = lam_re * lam_re + lam_im * lam_im
    q_re = (nr * lam_re + ni * lam_im) / den
    q_im = (ni * lam_re - nr * lam_im) / den
    bb_re = q_re[..., None] * b_re - q_im[..., None] * b_im
    bb_im = q_re[..., None] * b_im + q_im[..., None] * b_re
    jj = jnp.arange(tt + 1, dtype=F32)[None, :, None]
    pmag = jnp.exp(jj * (lam_re * dt)[:, None, :])
    pg_re = pmag * jnp.cos(jj * (lam_im * dt)[:, None, :])
    pg_im = pmag * jnp.sin(jj * (lam_im * dt)[:, None, :])
    ca = jnp.concatenate([c_re[:, None] * pg_re[:, :tt, None, :] - c_im[:, None] * pg_im[:, :tt, None, :],
                          -(c_re[:, None] * pg_im[:, :tt, None, :] + c_im[:, None] * pg_re[:, :tt, None, :])],
                         axis=-1)
    bb = jnp.concatenate([bb_re, bb_im], axis=1)
    kk_all = jnp.einsum('gjhp,gpk->gjhk', ca, bb, precision=HIGHEST)
    out = []
    for dd in range(nd):
        rev = dd == 1
        gs = slice(dd * g, (dd + 1) * g)
        kk = kk_all[gs]
        s_idx = jnp.arange(tt)[:, None]
        t_idx = jnp.arange(tt)[None, :]
        lag = (s_idx - t_idx) if rev else (t_idx - s_idx)
        sel = (lag[None] == jnp.arange(tt)[:, None, None]).astype(F32)
        m = jnp.einsum('jst,gjhk->gskth', sel, kk, precision=HIGHEST).reshape(g, tt * hh, tt * hh)
        pidx = jnp.arange(tt) if rev else (tt - 1 - jnp.arange(tt))
        pr, pi = pg_re[gs][:, pidx][:, :, None, :], pg_im[gs][:, pidx][:, :, None, :]
        br, bi = bb_re[gs].transpose(0, 2, 1)[:, None], bb_im[gs].transpose(0, 2, 1)[:, None]
        ab_re = (pr * br - pi * bi).reshape(g, tt * hh, p)
        ab_im = (pr * bi + pi * br).reshape(g, tt * hh, p)
        w = jnp.concatenate([ab_re, ab_im, ab_im, ab_re], axis=-1)
        ridx = (tt - jnp.arange(tt)) if rev else (1 + jnp.arange(tt))
        pr = pg_re[gs][:, ridx].transpose(0, 2, 1)[:, :, :, None]
        pi = pg_im[gs][:, ridx].transpose(0, 2, 1)[:, :, :, None]
        cr, ci = c_re[gs].transpose(0, 2, 1)[:, :, None, :], c_im[gs].transpose(0, 2, 1)[:, :, None, :]
        v_re = (cr * pr - ci * pi).reshape(g, p, tt * hh)
        v_im = (cr * pi + ci * pr).reshape(g, p, tt * hh)
        v = jnp.concatenate([v_re, -v_im], axis=1)
        at_re, at_im = pg_re[gs][:, tt], pg_im[gs][:, tt]
        a3 = jnp.stack([jnp.concatenate([at_re, at_re], -1),
                        jnp.concatenate([-at_im, at_im], -1),
                        jnp.concatenate([at_im, -at_im], -1)], axis=1)
        a3 = jnp.pad(a3, ((0, 0), (0, 5), (0, 0)))
        out.append((m.astype(BF16), w.astype(BF16), v.astype(BF16), a3))
    return out


def _s5_kernel(z_ref, d_ref, perm_ref, mf_ref, mb_ref, wf_ref, wb_ref, vf_ref, vb_ref, af_ref, ab_ref,
               h0f_ref, h0fs_ref, h0b_ref, h0bs_ref, y_ref, finf_ref, finb_ref,
               up_s, sa_f, sb_f, sa_b, sb_b, hin_f, hin_b, *, nb, nc, gpb, hh, tt):
    r = nb * nc
    cw = tt * hh
    p2 = hin_f.shape[-1]
    nt = (((1,), (1,)), ((), ()))
    xcat = jnp.concatenate([z_ref[pl.ds(t, r, stride=tt), :] for t in range(tt)], axis=-1)
    up_s[...] = jnp.dot(xcat.astype(BF16), perm_ref[...], preferred_element_type=F32).astype(BF16)
    for g in range(gpb):
        ug = up_s[:, g * cw:(g + 1) * cw]
        s4 = jnp.dot(ug, wf_ref[g], preferred_element_type=F32)
        sa_f[g] = s4[:, :p2]
        sb_f[g] = s4[:, p2:]
        s4 = jnp.dot(ug, wb_ref[g], preferred_element_type=F32)
        sa_b[g] = s4[:, :p2]
        sb_b[g] = s4[:, p2:]

    def scan(a_ref, sa, sb, hin, h0_ref, h0s_ref, fin_ref, reverse):
        def body(i, carry):
            c = (nc - 1 - i) if reverse else i
            rows = pl.ds(c, nb, stride=nc)
            new = []
            for g in range(gpb):
                hcur, hswp = carry[2 * g], carry[2 * g + 1]
                hin[g, rows, :] = hcur
                ar = a_ref[g, 0:1, :]
                new.append(hcur * ar + hswp * a_ref[g, 1:2, :] + sa[g, rows, :])
                new.append(hswp * ar + hcur * a_ref[g, 2:3, :] + sb[g, rows, :])
            return tuple(new)

        init = []
        for g in range(gpb):
            init += [h0_ref[0, g], h0s_ref[0, g]]
        fin = lax.fori_loop(0, nc, body, tuple(init))
        for g in range(gpb):
            fin_ref[0, g] = fin[2 * g]

    scan(af_ref, sa_f, sb_f, hin_f, h0f_ref, h0fs_ref, finf_ref, False)
    scan(ab_ref, sa_b, sb_b, hin_b, h0b_ref, h0bs_ref, finb_ref, True)
    ys = []
    for g in range(gpb):
        ug = up_s[:, g * cw:(g + 1) * cw]
        yg = jnp.dot(ug, mf_ref[g], preferred_element_type=F32)
        yg = yg + jnp.dot(ug, mb_ref[g], preferred_element_type=F32)
        yg = yg + jnp.dot(hin_f[g].astype(BF16), vf_ref[g], preferred_element_type=F32)
        yg = yg + jnp.dot(hin_b[g].astype(BF16), vb_ref[g], preferred_element_type=F32)
        ys.append(yg)
    ycat = jnp.concatenate(ys, axis=-1)
    hi = ycat.astype(BF16)
    lo = (ycat - hi.astype(F32)).astype(BF16)
    ytok = (lax.dot_general(hi, perm_ref[...], nt, preferred_element_type=F32)
            + lax.dot_general(lo, perm_ref[...], nt, preferred_element_type=F32))
    lanes = z_ref.shape[1]
    for t in range(tt):
        rows = pl.ds(t, r, stride=tt)
        y_ref[rows, :] = ytok[:, t * lanes:(t + 1) * lanes] + z_ref[rows, :] * d_ref[...]


def _s5_call(z, col0, dvec, perm, ops_f, ops_b, h0, nbatch, seq):
    g, hh = dvec.shape
    mf, wf, vf, af = ops_f
    mb, wb, vb, ab = ops_b
    p2 = vf.shape[1]
    tt = S5_CHUNK
    cw = tt * hh
    gpb = LANES // hh
    nblk = g // gpb
    nsplit = 2 if nbatch % 2 == 0 else 1
    nb = nbatch // nsplit
    nc = seq // tt
    r = nb * nc
    tok = r * tt
    cb0 = col0 // LANES
    kern = functools.partial(_s5_kernel, nb=nb, nc=nc, gpb=gpb, hh=hh, tt=tt)
    gblk = lambda shape: pl.BlockSpec((gpb,) + shape, lambda j, s: (j, 0, 0))
    hblk = pl.BlockSpec((1, gpb, nb, p2), lambda j, s: (s, j, 0, 0))
    return pl.pallas_call(
        kern,
        grid=(nblk, nsplit),
        in_specs=[pl.BlockSpec((tok, LANES), lambda j, s: (s, cb0 + j)),
                  pl.BlockSpec((1, LANES), lambda j, s: (0, j)),
                  pl.BlockSpec(perm.shape, lambda j, s: (0, 0), pipeline_mode=pl.Buffered(1)),
                  gblk((cw, cw)), gblk((cw, cw)), gblk((cw, 2 * p2)), gblk((cw, 2 * p2)),
                  gblk((p2, cw)), gblk((p2, cw)), gblk((8, p2)), gblk((8, p2)),
                  hblk, hblk, hblk, hblk],
        out_specs=[pl.BlockSpec((tok, LANES), lambda j, s: (s, j)), hblk, hblk],
        out_shape=[jax.ShapeDtypeStruct((nbatch * seq, g * hh), F32),
                   jax.ShapeDtypeStruct((nsplit, g, nb, p2), F32),
                   jax.ShapeDtypeStruct((nsplit, g, nb, p2), F32)],
        scratch_shapes=[pltpu.VMEM((r, gpb * cw), BF16)]
        + [pltpu.VMEM((gpb, r, p2), F32) for _ in range(6)],
        compiler_params=_cparams("arbitrary", "arbitrary"),
        name="s5",
    )(z, dvec.reshape(1, g * hh), perm, mf, mb, wf, wb, vf, vb, af, ab, *h0)


def _s5_perm(hh, tt):
    gpb = LANES // hh
    src = jnp.arange(tt * LANES)
    t, ch = src // LANES, src % LANES
    dst = (ch // hh) * (tt * hh) + t * hh + ch % hh
    return (dst[:, None] == jnp.arange(gpb * tt * hh)[None, :]).astype(BF16)


def _s5_branch(z, col0, nbatch, seq, ops_f, ops_b, dvec, perm, h0_re, h0_im):
    g = dvec.shape[0]
    p = ops_f[2].shape[1] // 2
    nsplit = 2 if nbatch % 2 == 0 else 1
    nb = nbatch // nsplit

    def init(d):
        if h0_re is None:
            zero = jnp.zeros((nsplit, g, nb, 2 * p), F32)
            return zero, zero
        hr = h0_re[:, d].reshape(nsplit, nb, g, p).transpose(0, 2, 1, 3)
        hi = h0_im[:, d].reshape(nsplit, nb, g, p).transpose(0, 2, 1, 3)
        return jnp.concatenate([hr, hi], -1), jnp.concatenate([hi, hr], -1)

    y, finf, finb = _s5_call(z, col0, dvec, perm, ops_f, ops_b, init(0) + init(1), nbatch, seq)
    fin = jnp.stack([finf, finb], axis=0)
    fin = fin.transpose(1, 3, 0, 2, 4).reshape(nbatch, 2, g, 2 * p)
    return y, fin[..., :p], fin[..., p:]


def _hg_chunk(qr, fz, v, lb, st_ref, *, rev, nheads, dk):
    cc = qr.shape[0]
    q = qr * _sigmoid(qr)
    t = jnp.exp2(jnp.abs(fz) * (-LOG2E))
    r = 1.0 / (1.0 + t)
    pos = fz >= 0.0
    sig = jnp.where(pos, r, t * r)
    nsig = jnp.where(pos, t * r, r)
    f = lb + (1.0 - lb) * sig
    k = (1.0 - lb) * nsig
    logf = jnp.log(f)
    ri = lax.broadcasted_iota(jnp.int32, (cc, cc), 0)
    ci = lax.broadcasted_iota(jnp.int32, (cc, cc), 1)
    tri = jnp.where((ci >= ri) if rev else (ci <= ri), 1.0, 0.0).astype(BF16)
    l_hi = logf.astype(BF16)
    r1 = logf - l_hi.astype(F32)
    l_mid = r1.astype(BF16)
    l_lo = (r1 - l_mid.astype(F32)).astype(BF16)
    cum = (jnp.dot(tri, l_hi, preferred_element_type=F32) + jnp.dot(tri, l_mid, preferred_element_type=F32)
           + jnp.dot(tri, l_lo, preferred_element_type=F32))
    last = cum[0:1, :] if rev else cum[cc - 1:cc, :]
    ecum_q = (q * jnp.exp(cum)).astype(BF16)
    elast_k = (k * jnp.exp(last - cum)).astype(BF16)
    elast = jnp.exp(last)
    vb = v.astype(BF16)
    rowi = lax.broadcasted_iota(jnp.int32, (cc, 1), 0)

    levels = []
    b = cc // 2
    while b >= 1:
        target = b if rev else b - 1
        upper = ((rowi // b) % 2) == 1
        qrow = jnp.logical_not(upper) if rev else upper
        if b >= 4:
            pieces = [jnp.broadcast_to(cum[i * 2 * b + target:i * 2 * b + target + 1, :], (2 * b, cum.shape[1]))
                      for i in range(cc // (2 * b))]
            rb = pieces[0] if len(pieces) == 1 else jnp.concatenate(pieces, axis=0)
            x = jnp.exp2(jnp.abs(cum - rb) * (-LOG2E))
        elif b == 2:
            pin = rowi % 4
            f_next = pltpu.roll(f, cc - 1, axis=0)
            f_prev = pltpu.roll(f, 1, axis=0)
            if rev:
                x = jnp.where(pin == 0, f * f_next, jnp.where(pin == 1, f, jnp.where(pin == 2, 1.0, f_prev)))
            else:
                x = jnp.where(pin == 0, f_next, jnp.where(pin == 1, 1.0, jnp.where(pin == 2, f, f * f_prev)))
        else:
            x = jnp.where(qrow, f, 1.0)
        y = (jnp.where(qrow, q, k) * x).astype(BF16)
        qr_m = ((ri // b) % 2) == (0 if rev else 1)
        kc_m = ((ci // b) % 2) == (1 if rev else 0)
        bm = jnp.where(((ri // (2 * b)) == (ci // (2 * b))) & qr_m & kc_m, 1.0, 0.0).astype(F32)
        levels.append((y, bm))
        b //= 2

    nt = (((1,), (1,)), ((), ()))
    outs = []
    for h in range(nheads):
        sl = slice(h * dk, (h + 1) * dk)
        att = None
        for y, bm in levels:
            yh = y[:, sl]
            a = lax.dot_general(yh, yh, nt, preferred_element_type=F32) * bm
            att = a if att is None else att + a
        dg = jnp.sum(q[:, sl] * k[:, sl], axis=-1, keepdims=True)
        st = st_ref[h]
        o = lax.dot_general(ecum_q[:, sl], st.astype(BF16), nt, preferred_element_type=F32)
        o = o + jnp.dot(att.astype(BF16), vb[:, sl], preferred_element_type=F32) + dg * v[:, sl]
        outs.append(o)
        vt = v[:, sl].T.astype(BF16)
        st_ref[h] = st * elast[:, sl] + jnp.dot(vt, elast_k[:, sl], preferred_element_type=F32)
    return outs


def _hg_load(ref, j, cc, gridded):
    if not gridded:
        return ref[0, j * cc:(j + 1) * cc, :]
    per = cc // ref.shape[1]
    return jnp.concatenate([ref[0, :, j * per + wi, :] for wi in range(per)], axis=0)


def _hg_store(ref, j, cc, gridded, val):
    if not gridded:
        ref[0, j * cc:(j + 1) * cc, :] = val
        return
    rows = ref.shape[1]
    for wi in range(cc // rows):
        ref[0, :, j * (cc // rows) + wi, :] = val[wi * rows:(wi + 1) * rows]


def _hgrn_kernel(*refs, rev, nheads, dk, cc, nsub, gridded, final, zero_init):
    refs = list(refs)
    q_ref, fz_ref, v_ref, lb_ref = refs[:4]
    del refs[:4]
    s0_ref = None if zero_init else refs.pop(0)
    if final:
        of_ref, g_ref, ng_ref = refs[:3]
        del refs[:3]
    o_ref, sfin_ref, st_ref = refs
    c = pl.program_id(1)

    @pl.when(c == 0)
    def _():
        st_ref[...] = jnp.zeros_like(st_ref) if s0_ref is None else s0_ref[0]

    lb = lb_ref[...]
    for jj in range(nsub):
        j = (nsub - 1 - jj) if rev else jj
        outs = _hg_chunk(_hg_load(q_ref, j, cc, gridded), _hg_load(fz_ref, j, cc, gridded),
                         _hg_load(v_ref, j, cc, gridded), lb, st_ref, rev=rev, nheads=nheads, dk=dk)
        if final:
            of = _hg_load(of_ref, j, cc, gridded)
            g = _hg_load(g_ref, j, cc, gridded)
            normed = []
            for h in range(nheads):
                o = outs[h] + of[:, h * dk:(h + 1) * dk]
                ms = jnp.mean(o * o, axis=-1, keepdims=True)
                normed.append(o * lax.rsqrt(ms + LN_EPS))
            val = jnp.concatenate(normed, axis=-1) * ng_ref[...] * (g * _sigmoid(g))
        else:
            val = jnp.concatenate(outs, axis=-1)
        _hg_store(o_ref, j, cc, gridded, val)

    @pl.when(c == pl.num_programs(1) - 1)
    def _():
        sfin_ref[0] = st_ref[...]


def _hgrn_call(z, cols, fz_key, lb, s0t, state_shape, nbatch, seq, rows, rev, o_fwd=None, ng=None):
    nheads, dv, dk = state_shape
    w = nheads * dk
    ncol = z.shape[1]
    cc = min(HG_CHUNK, seq)
    nsub = 2 if seq % (2 * cc) == 0 else 1
    blk = nsub * cc
    nc = seq // blk
    final = o_fwd is not None
    gridded = rows is not None
    cidx = (lambda c: nc - 1 - c) if rev else (lambda c: c)
    if gridded:
        wcols = blk // rows
        zv = z.reshape(nbatch, rows, GRID_W, ncol)
        spec = lambda cb: pl.BlockSpec((1, rows, wcols, w), lambda b, c: (b, 0, cidx(c), cb))
        ov = None if o_fwd is None else o_fwd.reshape(nbatch, rows, GRID_W, w)
        oshape = (nbatch, rows, GRID_W, w)
    else:
        zv = z.reshape(nbatch, seq, ncol)
        spec = lambda cb: pl.BlockSpec((1, blk, w), lambda b, c: (b, cidx(c), cb))
        ov = None if o_fwd is None else o_fwd.reshape(nbatch, seq, w)
        oshape = (nbatch, seq, w)
    sspec = pl.BlockSpec((1, nheads, dv, dk), lambda b, c: (b, 0, 0, 0))
    vspec = pl.BlockSpec((1, w), lambda b, c: (0, 0))
    in_specs = [spec(cols['q'] // w), spec(cols[fz_key] // w), spec(cols['v'] // w), vspec]
    args = [zv, zv, zv, lb]
    if s0t is not None:
        in_specs.append(sspec)
        args.append(s0t)
    if final:
        in_specs += [spec(0), spec(cols['g'] // w), vspec]
        args += [ov, zv, ng]
    kern = functools.partial(_hgrn_kernel, rev=rev, nheads=nheads, dk=dk, cc=cc, nsub=nsub, gridded=gridded,
                             final=final, zero_init=s0t is None)
    o, sfin = pl.pallas_call(
        kern,
        grid=(nbatch, nc),
        in_specs=in_specs,
        out_specs=[spec(0), sspec],
        out_shape=[jax.ShapeDtypeStruct(oshape, F32),
                   jax.ShapeDtypeStruct((nbatch, nheads, dv, dk), F32)],
        scratch_shapes=[pltpu.VMEM((nheads, dv, dk), F32)],
        compiler_params=_cparams("arbitrary", "arbitrary"),
        name="hgrn_bwd" if rev else "hgrn_fwd",
    )(*args)
    return o.reshape(nbatch * seq, w), sfin


def _merge1_kernel(y_ref, yb_ref, ga_ref, gb_ref, wglu_ref, bglu_ref, wa_ref, wb_ref, m_ref, ya_s, yb_s):
    @pl.when(pl.program_id(1) == 0)
    def _():
        y = jax.nn.gelu(y_ref[...])
        glu = jnp.dot(y.astype(BF16), wglu_ref[...], preferred_element_type=F32) + bglu_ref[...]
        ya_s[...] = (y * _sigmoid(glu)).astype(BF16)
        yb_s[...] = yb_ref[...].astype(BF16)

    pa = jnp.dot(ya_s[...], wa_ref[...], preferred_element_type=F32)
    pb = jnp.dot(yb_s[...], wb_ref[...], preferred_element_type=F32)
    m_ref[...] = (_sigmoid(ga_ref[...]) * pa + _sigmoid(gb_ref[...]) * pb).astype(BF16)


def _merge1_call(y_s5, y_hg, z, cols, wglu, bglu, wa, wb):
    t, sw = y_s5.shape
    hw = y_hg.shape[1]
    d = wa.shape[1]
    tm = _pick(t, 512, 8)
    tn = _pick(math.gcd(d, cols['ga']), 1024)
    nj = d // tn
    ga0 = cols['ga'] // tn
    gb0 = cols['gb'] // tn
    return pl.pallas_call(
        _merge1_kernel,
        grid=(t // tm, nj),
        in_specs=[pl.BlockSpec((tm, sw), lambda i, j: (i, 0)),
                  pl.BlockSpec((tm, hw), lambda i, j: (i, 0)),
                  pl.BlockSpec((tm, tn), lambda i, j: (i, ga0 + j)),
                  pl.BlockSpec((tm, tn), lambda i, j: (i, gb0 + j)),
                  pl.BlockSpec((sw, sw), lambda i, j: (0, 0)),
                  pl.BlockSpec((1, sw), lambda i, j: (0, 0)),
                  pl.BlockSpec((sw, tn), lambda i, j: (0, j)),
                  pl.BlockSpec((hw, tn), lambda i, j: (0, j))],
        out_specs=pl.BlockSpec((tm, tn), lambda i, j: (i, j)),
        out_shape=jax.ShapeDtypeStruct((t, d), BF16),
        scratch_shapes=[pltpu.VMEM((tm, sw), BF16), pltpu.VMEM((tm, hw), BF16)],
        compiler_params=_cparams("arbitrary", "arbitrary"),
        name="merge1",
    )(y_s5, y_hg, z, z, wglu, bglu, wa, wb)


def _outproj_kernel(m_ref, w_ref, x_ref, g1_ref, t_ref, *, tiles_per_seq, mod_base, alpha):
    row = _mod_row(pl.program_id(0), tiles_per_seq, mod_base)
    mix = jnp.dot(m_ref[...], w_ref[...], preferred_element_type=F32)
    t_ref[...] = alpha * x_ref[...] + g1_ref[0, pl.ds(row, 1), :] * mix


def _outproj_call(m, w_out, x, mods, seq_len, mod_base, alpha):
    t, d = x.shape
    tm = _pick(seq_len if mod_base else t, 1024, 8)
    tn = _pick(d, 1024)
    tiles_per_seq = (seq_len // tm) if mod_base else (t // tm)
    kern = functools.partial(_outproj_kernel, tiles_per_seq=tiles_per_seq, mod_base=mod_base, alpha=alpha)
    return pl.pallas_call(
        kern,
        grid=(t // tm, d // tn),
        in_specs=[pl.BlockSpec((tm, d), lambda i, j: (i, 0)),
                  pl.BlockSpec((d, tn), lambda i, j: (0, j)),
                  pl.BlockSpec((tm, tn), lambda i, j: (i, j)),
                  pl.BlockSpec((1, mods.shape[1], tn), lambda i, j: (2, 0, j))],
        out_specs=pl.BlockSpec((tm, tn), lambda i, j: (i, j)),
        out_shape=jax.ShapeDtypeStruct((t, d), F32),
        compiler_params=_cparams("arbitrary", "arbitrary"),
        name="outproj",
    )(m, w_out, x, mods)


def _post_kernel(t_ref, mod_ref, lng_ref, lnb_ref, wr_ref, x1_ref, h2_ref, lg_ref, *, tiles_per_seq, mod_base):
    row = _mod_row(pl.program_id(0), tiles_per_seq, mod_base)
    sh2 = mod_ref[3, pl.ds(row, 1), :]
    sc2 = mod_ref[4, pl.ds(row, 1), :]
    x1 = _layer_norm(t_ref[...]) * lng_ref[...] + lnb_ref[...]
    x1_ref[...] = x1
    h2 = _layer_norm(x1) * (1.0 + sc2) + sh2
    h_hi = h2.astype(BF16)
    h2_ref[...] = h_hi
    h_mid = (h2 - h_hi.astype(F32)).astype(BF16)
    hw2 = jnp.dot(h_hi, wr_ref[...], preferred_element_type=F32)
    lg_ref[...] = (hw2[:, :LANES] + hw2[:, LANES:]
                   + jnp.dot(h_mid, wr_ref[:, :LANES], preferred_element_type=F32))


def _post_call(tres, mods, lng, lnb, wr, seq_len, mod_base):
    t, d = tres.shape
    tm = _pick(seq_len if mod_base else t, 256, 8)
    tiles_per_seq = (seq_len // tm) if mod_base else (t // tm)
    kern = functools.partial(_post_kernel, tiles_per_seq=tiles_per_seq, mod_base=mod_base)
    return pl.pallas_call(
        kern,
        grid=(t // tm,),
        in_specs=[pl.BlockSpec((tm, d), lambda i: (i, 0)),
                  pl.BlockSpec(mods.shape, lambda i: (0, 0, 0)),
                  pl.BlockSpec((1, d), lambda i: (0, 0)),
                  pl.BlockSpec((1, d), lambda i: (0, 0)),
                  pl.BlockSpec((d, 2 * LANES), lambda i: (0, 0))],
        out_specs=[pl.BlockSpec((tm, d), lambda i: (i, 0)),
                   pl.BlockSpec((tm, d), lambda i: (i, 0)),
                   pl.BlockSpec((tm, LANES), lambda i: (i, 0))],
        out_shape=[jax.ShapeDtypeStruct((t, d), F32),
                   jax.ShapeDtypeStruct((t, d), BF16),
                   jax.ShapeDtypeStruct((t, LANES), F32)],
        compiler_params=_cparams("arbitrary"),
        name="post",
    )(tres, mods, lng, lnb, wr)


def _route_kernel(lg_ref, pos_ref, posc_ref, gatec_ref, aff_s, afft_s, cnt_s, *, ne, cap):
    n = lg_ref.shape[0]
    lg = lg_ref[...]
    lane = lax.broadcasted_iota(jnp.int32, (n, LANES), 1)
    lmask = jnp.where(lane < ne, lg, -jnp.inf)
    mx = jnp.max(lmask, axis=-1, keepdims=True)
    ex = jnp.exp(lmask - mx)
    aff = ex / jnp.sum(ex, axis=-1, keepdims=True)
    aff_s[...] = aff
    afft_s[...] = aff.T
    cnt_s[...] = jnp.zeros_like(cnt_s)
    sb = min(n, 64)
    tl = lax.broadcasted_iota(jnp.int32, (1, n), 1)

    def count(i, carry):
        s0 = pl.multiple_of(i * sb, sb)
        blk = aff_s[pl.ds(s0, sb), :]
        si = s0 + lax.broadcasted_iota(jnp.int32, (sb, 1), 0)
        tie = jnp.where(si < tl, 1.0, 0.0)
        for e in range(ne):
            col = blk[:, e:e + 1]
            row = afft_s[e:e + 1, :]
            beats = jnp.where(col > row, 1.0, jnp.where(col == row, tie, 0.0))
            cnt_s[e:e + 1, :] += jnp.sum(beats, axis=0, keepdims=True)
        return carry

    lax.fori_loop(0, n // sb, count, 0)
    sel = jnp.where(cnt_s[...] < cap, 1.0, 0.0)
    selb = sel.astype(BF16)
    pb = min(n, 256)
    ui = lax.broadcasted_iota(jnp.int32, (n, pb), 0)
    uj = lax.broadcasted_iota(jnp.int32, (n, pb), 1)
    pos_blocks = []
    for t0 in range(0, n, pb):
        upper = jnp.where(ui < uj + t0, 1.0, 0.0).astype(BF16)
        pos_blocks.append(jnp.dot(selb, upper, preferred_element_type=F32))
    pos = jnp.concatenate(pos_blocks, axis=-1) if len(pos_blocks) > 1 else pos_blocks[0]
    pos = jnp.where(sel > 0.0, pos, -1.0)
    gate = sel * afft_s[0:ne, :]
    pos_ref[0] = pos.astype(jnp.int32)
    pad = jnp.zeros((LANES - ne, n), F32)
    posc_ref[...] = jnp.concatenate([pos, pad], axis=0).T
    gatec_ref[...] = jnp.concatenate([gate, pad], axis=0).T


def _route_call(lg, nsets, n, ne, cap):
    kern = functools.partial(_route_kernel, ne=ne, cap=cap)
    return pl.pallas_call(
        kern,
        grid=(nsets,),
        in_specs=[pl.BlockSpec((n, LANES), lambda s: (s, 0))],
        out_specs=[pl.BlockSpec((1, ne, n), lambda s: (s, 0, 0)),
                   pl.BlockSpec((n, LANES), lambda s: (s, 0)),
                   pl.BlockSpec((n, LANES), lambda s: (s, 0))],
        out_shape=[jax.ShapeDtypeStruct((nsets, ne, n), jnp.int32),
                   jax.ShapeDtypeStruct((nsets * n, LANES), F32),
                   jax.ShapeDtypeStruct((nsets * n, LANES), F32)],
        scratch_shapes=[pltpu.VMEM((n, LANES), F32), pltpu.VMEM((LANES, n), F32), pltpu.VMEM((ne, n), F32)],
        compiler_params=_cparams("arbitrary"),
        name="route",
    )(lg)


def _gather_kernel(pos_ref, h_ref, x_ref, *, ne, cap):
    n = h_ref.shape[0]
    pos = pos_ref[0]
    h = h_ref[...]
    jidx = lax.broadcasted_iota(jnp.int32, (cap, n), 0)
    onehot = lambda e: jnp.where(pos[e:e + 1, :] == jidx, 1.0, 0.0).astype(BF16)
    if cap * ne <= 1024:
        p = jnp.concatenate([onehot(e) for e in range(ne)], axis=0)
        x = jnp.dot(p, h, preferred_element_type=F32).astype(BF16)
        for e in range(ne):
            x_ref[e] = x[e * cap:(e + 1) * cap]
    else:
        for e in range(ne):
            x_ref[e] = jnp.dot(onehot(e), h, preferred_element_type=F32).astype(BF16)


def _gather_call(pos, h2, nsets, n, ne, cap):
    d = h2.shape[1]
    td = _pick(d, 4096 if n <= 512 else 1024)
    kern = functools.partial(_gather_kernel, ne=ne, cap=cap)
    return pl.pallas_call(
        kern,
        grid=(nsets, d // td),
        in_specs=[pl.BlockSpec((1, ne, n), lambda s, j: (s, 0, 0)),
                  pl.BlockSpec((n, td), lambda s, j: (s, j))],
        out_specs=pl.BlockSpec((ne, cap, td), lambda s, j: (0, s, j)),
        out_shape=jax.ShapeDtypeStruct((ne, nsets * cap, d), BF16),
        compiler_params=_cparams("arbitrary", "arbitrary"),
        name="gather",
    )(pos, h2)


def _ffn_kernel(x_ref, wg_ref, wu_ref, wd_ref, y_ref, hid_ref, *, nf):
    j = pl.program_id(1)

    @pl.when(j < nf)
    def _():
        x = x_ref[0]
        a = jnp.dot(x, wg_ref[0].astype(BF16), preferred_element_type=F32)
        b = jnp.dot(x, wu_ref[0].astype(BF16), preferred_element_type=F32)
        hid = (a * _sigmoid(a) * b).astype(BF16)
        tf = hid.shape[1]
        for jj in range(nf):

            @pl.when(j == jj)
            def _():
                hid_ref[:, jj * tf:(jj + 1) * tf] = hid

    @pl.when(j >= nf)
    def _():
        y_ref[0] = jnp.dot(hid_ref[...], wd_ref[0].astype(BF16), preferred_element_type=F32).astype(BF16)


def _ffn_call(xg, w_gate, w_up, w_down):
    ne, r, d = xg.shape
    ff = w_gate.shape[2]
    tf = _pick(ff, 256)
    tn = _pick(d, 256)
    nf = ff // tf
    nn = d // tn
    kern = functools.partial(_ffn_kernel, nf=nf)
    return pl.pallas_call(
        kern,
        grid=(ne, nf + nn),
        in_specs=[pl.BlockSpec((1, r, d), lambda e, j: (e, 0, 0), pipeline_mode=pl.Buffered(1)),
                  pl.BlockSpec((1, d, tf), lambda e, j: (e, 0, jnp.minimum(j, nf - 1))),
                  pl.BlockSpec((1, d, tf), lambda e, j: (e, 0, jnp.minimum(j, nf - 1))),
                  pl.BlockSpec((1, ff, tn), lambda e, j: (e, 0, jnp.maximum(j - nf, 0)))],
        out_specs=pl.BlockSpec((1, r, tn), lambda e, j: (e, 0, jnp.maximum(j - nf, 0))),
        out_shape=jax.ShapeDtypeStruct((ne, r, d), BF16),
        scratch_shapes=[pltpu.VMEM((r, ff), BF16)],
        compiler_params=_cparams("arbitrary", "arbitrary"),
        name="ffn",
    )(xg, w_gate, w_up, w_down)


def _combine_kernel(posc_ref, gatec_ref, y_ref, x1_ref, mod_ref, lng_ref, lnb_ref, o_ref, acc_ref,
                    *, ke, cap, nk, tiles_per_seq, mod_base, alpha):
    k = pl.program_id(2)

    @pl.when(k == 0)
    def _():
        acc_ref[...] = jnp.zeros_like(acc_ref)

    lane = lax.broadcasted_iota(jnp.int32, (1, LANES), 1)
    posc = posc_ref[...]
    gatec = gatec_ref[...]

    def column(ee):
        m = lane == k * ke + ee
        return (jnp.sum(jnp.where(m, posc, 0.0), axis=-1, keepdims=True),
                jnp.sum(jnp.where(m, gatec, 0.0), axis=-1, keepdims=True))

    pieces = []
    if cap >= LANES:
        for ee in range(ke):
            pc, gc = column(ee)
            for c0 in range(0, cap, LANES):
                pieces.append(jnp.where(pc == (lane + c0).astype(F32), gc, 0.0))
    else:
        per = LANES // cap
        for e0 in range(0, ke, per):
            pb, gb = column(e0)
            for q in range(1, per):
                pq, gq = column(e0 + q)
                inq = lane >= q * cap
                pb = jnp.where(inq, pq, pb)
                gb = jnp.where(inq, gq, gb)
            pieces.append(jnp.where(pb == (lane % cap).astype(F32), gb, 0.0))
    gt = (jnp.concatenate(pieces, axis=-1) if len(pieces) > 1 else pieces[0]).astype(BF16)
    y = y_ref[...].reshape(ke * cap, y_ref.shape[-1])
    acc_ref[...] += jnp.dot(gt, y, preferred_element_type=F32)

    @pl.when(k == nk - 1)
    def _():
        row = _mod_row(pl.program_id(0) * pl.num_programs(1) + pl.program_id(1), tiles_per_seq, mod_base)
        g2 = mod_ref[5, pl.ds(row, 1), :]
        o_ref[...] = _layer_norm(alpha * x1_ref[...] + g2 * acc_ref[...]) * lng_ref[...] + lnb_ref[...]


def _combine_call(posc, gatec, y, x1, mods, lng, lnb, nsets, n, ne, cap, mod_base, alpha):
    t, d = x1.shape
    tq = _pick(n, 256, 8)
    nti = n // tq
    ke = max(1, min(ne, 1024 // cap))
    nk = ne // ke
    tiles_per_seq = nti if mod_base else (t // tq)
    kern = functools.partial(_combine_kernel, ke=ke, cap=cap, nk=nk, tiles_per_seq=tiles_per_seq,
                             mod_base=mod_base, alpha=alpha)
    return pl.pallas_call(
        kern,
        grid=(nsets, nti, nk),
        in_specs=[pl.BlockSpec((tq, LANES), lambda s, i, k: (s * nti + i, 0)),
                  pl.BlockSpec((tq, LANES), lambda s, i, k: (s * nti + i, 0)),
                  pl.BlockSpec((ke, cap, d), lambda s, i, k: (k, s, 0)),
                  pl.BlockSpec((tq, d), lambda s, i, k: (s * nti + i, 0)),
                  pl.BlockSpec(mods.shape, lambda s, i, k: (0, 0, 0)),
                  pl.BlockSpec((1, d), lambda s, i, k: (0, 0)),
                  pl.BlockSpec((1, d), lambda s, i, k: (0, 0))],
        out_specs=pl.BlockSpec((tq, d), lambda s, i, k: (s * nti + i, 0)),
        out_shape=jax.ShapeDtypeStruct((t, d), F32),
        scratch_shapes=[pltpu.VMEM((tq, d), F32)],
        compiler_params=_cparams("arbitrary", "arbitrary", "arbitrary"),
        name="combine",
    )(posc, gatec, y, x1, mods, lng, lnb)


def _group_forward(x, nbatch, seq, mod_base, rows, mods, prm, s5_h0, hg_s0t, alpha):
    d = x.shape[-1]
    x2 = x.reshape(nbatch * seq, d)
    cols = prm['cols']
    z = _lnmm_call(x2, mods, prm['w_in'], seq, mod_base, prm['col_shift'])

    y_s5, fin_re, fin_im = _s5_branch(z, cols['s5'], nbatch, seq, prm['s5_f'], prm['s5_b'], prm['s5_d'], prm['s5_perm'],
                                      None if s5_h0 is None else s5_h0[0], None if s5_h0 is None else s5_h0[1])

    s0f, s0b = (None, None) if hg_s0t is None else (hg_s0t[:, 0], hg_s0t[:, 1])
    o_f, sfin_f = _hgrn_call(z, cols, 'ff', prm['lb'][0], s0f, prm['hg_state'], nbatch, seq, rows, False)
    y_hg, sfin_b = _hgrn_call(z, cols, 'fb', prm['lb'][1], s0b, prm['hg_state'], nbatch, seq, rows, True,
                              o_fwd=o_f, ng=prm['hg_norm_g'])

    m = _merge1_call(y_s5, y_hg, z, cols, prm['w_glu'], prm['b_glu'], prm['w_proj_a'], prm['w_proj_b'])
    tres = _outproj_call(m, prm['w_out'], x2, mods, seq, mod_base, alpha)
    x1, h2, lg = _post_call(tres, mods, prm['ln1_g'], prm['ln1_b'], prm['w_router'], seq, mod_base)
    ne = prm['ne']
    cap = EC_FACTOR * seq // ne
    pos, posc, gatec = _route_call(lg, nbatch, seq, ne, cap)
    xg = _gather_call(pos, h2, nbatch, seq, ne, cap)
    yg = _ffn_call(xg, prm['w_gate'], prm['w_up'], prm['w_down'])
    out = _combine_call(posc, gatec, yg, x1, mods, prm['ln2_g'], prm['ln2_b'], nbatch, seq, ne, cap,
                        mod_base, alpha)
    hg_fin = jnp.stack([sfin_f, sfin_b], axis=1)
    return out.reshape(nbatch, seq, d), fin_re, fin_im, jnp.swapaxes(hg_fin, -1, -2)


def kernel(x_prompt, x_sample, state_s5_re, state_s5_im, state_hgrn, c, c_ctx, w_ada, b_ada, w_in, s5_lambda_re, s5_lambda_im, s5_log_step, s5_b_re, s5_b_im, s5_c_re, s5_c_im, s5_d, s5_w_glu, s5_b_glu, hg_lower_bounds, hg_norm_g, w_proj_a, w_proj_b, w_out, ln1_g, ln1_b, w_router, w_gate, w_up, w_down, ln2_g, ln2_b):
    depth = w_ada.shape[0]
    assert depth == 1, "single shared layer"
    nb_p, seq_p, d = x_prompt.shape
    nb_s, seq_s, _ = x_sample.shape
    g, p = s5_lambda_re.shape[2:]
    sw = s5_w_glu.shape[1]
    nheads, dk, dv = state_hgrn.shape[3:]
    hw = nheads * dk
    ne = w_gate.shape[1]
    alpha = (2 * depth) ** 0.25
    l = 0

    cond = jnp.concatenate([c_ctx[None], c, jnp.zeros((8 - 1 - nb_s, d), F32)], axis=0)
    mods = _ada_call(cond, w_ada[l], b_ada[l][None])
    mods = mods.reshape(8, 6, d).transpose(1, 0, 2)

    w_in_b = w_in[l].astype(BF16)
    cols = {'q': 0, 'ff': hw, 'fb': 2 * hw, 'v': 3 * hw, 'g': 4 * hw, 'ga': 5 * hw, 'gb': 5 * hw + d,
            's5': 5 * hw + 2 * d}

    lb_all = jnp.cumsum(jax.nn.softmax(hg_lower_bounds.astype(F32), axis=1), axis=1)[:, l]
    ops = _s5_operators(s5_lambda_re[l], s5_lambda_im[l], s5_log_step[l], s5_b_re[l], s5_b_im[l],
                        s5_c_re[l], s5_c_im[l])
    wr = jnp.pad(w_router[l], ((0, 0), (0, LANES - ne)))
    wr_hi = wr.astype(BF16)
    wr_split = jnp.concatenate([wr_hi, (wr - wr_hi.astype(F32)).astype(BF16)], axis=1)
    prm = {
        'cols': cols, 'ne': ne, 'hg_state': (nheads, dv, dk), 'col_shift': sw,
        'w_in': w_in_b, 's5_f': ops[0], 's5_b': ops[1], 's5_d': s5_d[l], 's5_perm': _s5_perm(s5_d.shape[-1], S5_CHUNK),
        'lb': lb_all[:, None, :],
        'w_glu': s5_w_glu[l].astype(BF16), 'b_glu': s5_b_glu[l][None], 'hg_norm_g': hg_norm_g[l][None],
        'w_proj_a': w_proj_a[l].astype(BF16), 'w_proj_b': w_proj_b[l].astype(BF16),
        'w_out': w_out[l].astype(BF16), 'ln1_g': ln1_g[l][None], 'ln1_b': ln1_b[l][None],
        'w_router': wr_split,
        'w_gate': w_gate[l], 'w_up': w_up[l], 'w_down': w_down[l],
        'ln2_g': ln2_g[l][None], 'ln2_b': ln2_b[l][None],
    }
    yp, re_p, im_p, hg_p = _group_forward(x_prompt, nb_p, seq_p, 0, None, mods, prm, None, None, alpha)
    s0t = jnp.swapaxes(state_hgrn[:, l], -1, -2)
    ys, _, _, _ = _group_forward(x_sample, nb_s, seq_s, 1, seq_s // GRID_W, mods, prm,
                                 (state_s5_re[:, l], state_s5_im[:, l]), s0t, alpha)
    return yp, ys, re_p[:, None], im_p[:, None], hg_p[:, None]
```
